```python
import math
import jax
import jax.numpy as jnp
from jax import lax
import numpy as np

D_MODEL = 1024
BATCH = 4
SEQ = 4096
DEPTH = 2
DEC_BATCH = 32
DEC_SEQ = 1
PAST_LEN = 16384
PAGE_SIZE = 128

N_META = 16
HEAD_DIM = 64
W_MIX = D_MODEL
W_A = W_MIX // 2
W_B = W_MIX - W_A
H_A = W_A // (2 * HEAD_DIM)
H_B = W_B // HEAD_DIM
D_FF = 4 * D_MODEL
Q_BLOCK = 128
OFF_QA = 0
OFF_KA = W_A
OFF_VA = 2 * W_A
OFF_QB = 3 * W_A
OFF_KB = 3 * W_A + W_B
OFF_VB = 3 * W_A + 2 * W_B
OFF_F = 3 * W_A + 3 * W_B
PROJ_W = 3 * W_A + 3 * W_B + H_B
RMS_EPS = 1e-6
NEG_INF = -1e30
F_BIAS_CENTER = 2.0

kernel_name = "hymba_diff_fox_decoder_step"


def rmsnorm(x, g):
    xf = x.astype(jnp.float32)
    y = xf * lax.rsqrt(jnp.mean(xf * xf, axis=-1, keepdims=True) + RMS_EPS)
    return (y * g.astype(jnp.float32)).astype(x.dtype)


def alibi_slopes():
    return jnp.exp2(-8.0 * jnp.arange(1, H_A + 1, dtype=jnp.float32) / H_A)


def lambda_init(l):
    return 0.8 - 0.6 * math.exp(-0.3 * l)


def diff_lambda(lq1, lk1, lq2, lk2, l):
    f32 = jnp.float32
    return (jnp.exp(jnp.sum(lq1.astype(f32) * lk1.astype(f32)))
            - jnp.exp(jnp.sum(lq2.astype(f32) * lk2.astype(f32))) + lambda_init(l))


def project(h, w_in_l, b_f_l, qn_a_l, kn_a_l, qn_b_l, kn_b_l):
    B, T, _ = h.shape
    p = h @ w_in_l
    qa = p[..., OFF_QA:OFF_KA].reshape(B, T, H_A, 2, HEAD_DIM)
    ka = p[..., OFF_KA:OFF_VA].reshape(B, T, H_A, 2, HEAD_DIM)
    va = p[..., OFF_VA:OFF_QB].reshape(B, T, H_A, 2 * HEAD_DIM)
    qb = p[..., OFF_QB:OFF_KB].reshape(B, T, H_B, HEAD_DIM)
    kb = p[..., OFF_KB:OFF_VB].reshape(B, T, H_B, HEAD_DIM)
    vb = p[..., OFF_VB:OFF_F].reshape(B, T, H_B, HEAD_DIM)
    logf = jax.nn.log_sigmoid(p[..., OFF_F:].astype(jnp.float32) + b_f_l.astype(jnp.float32))
    qa = rmsnorm(qa, qn_a_l)
    ka = rmsnorm(ka, kn_a_l)
    qb = rmsnorm(qb, qn_b_l)
    kb = rmsnorm(kb, kn_b_l)
    return qa, ka, va, qb, kb, vb, logf


def diff_attend(q, k, v, q_pos, k_pos, lam):
    f32 = jnp.float32
    s = jnp.einsum("bqhmd,bkhmd->bhmqk", q.astype(f32), k.astype(f32)) * (HEAD_DIM ** -0.5)
    dist = (q_pos[:, None] - k_pos[None, :]).astype(f32)
    s = s - alibi_slopes()[:, None, None, None] * dist
    s = jnp.where(dist >= 0, s, NEG_INF)
    p = jax.nn.softmax(s, axis=-1)
    p = p[:, :, 0] - lam * p[:, :, 1]
    return jnp.einsum("bhqk,bkhe->bqhe", p, v.astype(f32))


def fox_attend(q, k, v, q_pos, k_pos, cum_q, cum_k):
    f32 = jnp.float32
    s = jnp.einsum("bqhd,bkhd->bhqk", q.astype(f32), k.astype(f32)) * (HEAD_DIM ** -0.5)
    s = s + jnp.swapaxes(cum_q, 1, 2)[..., :, None] - jnp.swapaxes(cum_k, 1, 2)[..., None, :]
    s = jnp.where(q_pos[:, None] >= k_pos[None, :], s, NEG_INF)
    p = jax.nn.softmax(s, axis=-1)
    return jnp.einsum("bhqk,bkhd->bqhd", p, v.astype(f32))


def prompt_mix(qa, ka, va, qb, kb, vb, logf, lam):
    B, L = qa.shape[0], qa.shape[1]
    pos = jnp.arange(L)
    cum = jnp.cumsum(logf, axis=1)

    def block(start, size):
        sl = lambda a: lax.dynamic_slice_in_dim(a, start, size, axis=1)
        q_pos = lax.dynamic_slice_in_dim(pos, start, size)
        oa = diff_attend(sl(qa), ka, va, q_pos, pos, lam)
        ob = fox_attend(sl(qb), kb, vb, q_pos, pos, sl(cum), cum)
        return oa, ob

    oa_meta, ob_meta = block(0, N_META)
    n_blocks = (L - N_META) // Q_BLOCK
    oa_blk, ob_blk = lax.map(lambda b: block(N_META + b * Q_BLOCK, Q_BLOCK), jnp.arange(n_blocks))

    def merge(o_meta, o_blk):
        o_blk = jnp.moveaxis(o_blk, 0, 1)
        o_blk = o_blk.reshape((B, n_blocks * Q_BLOCK) + o_blk.shape[3:])
        return jnp.concatenate([o_meta, o_blk], axis=1)

    return merge(oa_meta, oa_blk), merge(ob_meta, ob_blk)


def gather_pages(pool, l, page_table):
    g = pool[l, page_table]
    return g.reshape((g.shape[0], g.shape[1] * g.shape[2]) + g.shape[3:])


def sample_mix(qa, ka, va, qb, kb, vb, logf, lam, l, cache_a_k, cache_a_v, cache_b_k, cache_b_v,
               cache_b_logf, page_table):
    DB, T = qa.shape[0], qa.shape[1]
    past = page_table.shape[1] * PAGE_SIZE
    ka_all = jnp.concatenate([gather_pages(cache_a_k, l, page_table).reshape(DB, past, H_A, 2, HEAD_DIM).astype(ka.dtype), ka], axis=1)
    va_all = jnp.concatenate([gather_pages(cache_a_v, l, page_table).astype(va.dtype), va], axis=1)
    kb_all = jnp.concatenate([gather_pages(cache_b_k, l, page_table).astype(kb.dtype), kb], axis=1)
    vb_all = jnp.concatenate([gather_pages(cache_b_v, l, page_table).astype(vb.dtype), vb], axis=1)
    logf_all = jnp.concatenate([gather_pages(cache_b_logf, l, page_table).astype(jnp.float32), logf], axis=1)
    cum = jnp.cumsum(logf_all, axis=1)
    q_pos = past + jnp.arange(T)
    k_pos = jnp.arange(past + T)
    oa = diff_attend(qa, ka_all, va_all, q_pos, k_pos, lam)
    ob = fox_attend(qb, kb_all, vb_all, q_pos, k_pos, cum[:, past:], cum)
    return oa, ob


def finish(x, oa, ob, l, subln_l, w_o_l, norm2_l, w_up_l, w_down_l):
    B, T, _ = x.shape
    oa = rmsnorm(oa, subln_l) * (1.0 - lambda_init(l))
    o = jnp.concatenate([oa.reshape(B, T, W_A), ob.reshape(B, T, W_B)], axis=-1).astype(x.dtype)
    x = x + o @ w_o_l
    u = jax.nn.relu(rmsnorm(x, norm2_l) @ w_up_l)
    return x + (u * u) @ w_down_l


def setup_inputs(seed: int = 0) -> dict:
    key = jax.random.key(seed)
    ks = jax.random.split(key, 26)
    nrm = jax.random.normal
    n_pages = PAST_LEN // PAGE_SIZE
    n_used = DEC_BATCH * n_pages
    n_pool = n_used + n_used // 4
    x_prompt = nrm(ks[0], (BATCH, SEQ, D_MODEL), jnp.float32)
    x_sample = nrm(ks[1], (DEC_BATCH, DEC_SEQ, D_MODEL), jnp.float32)
    cache_a_k = nrm(ks[2], (DEPTH, n_pool, PAGE_SIZE, H_A, 2 * HEAD_DIM), jnp.float32)
    cache_a_v = nrm(ks[3], (DEPTH, n_pool, PAGE_SIZE, H_A, 2 * HEAD_DIM), jnp.float32)
    cache_b_k = nrm(ks[4], (DEPTH, n_pool, PAGE_SIZE, H_B, HEAD_DIM), jnp.float32)
    cache_b_v = nrm(ks[5], (DEPTH, n_pool, PAGE_SIZE, H_B, HEAD_DIM), jnp.float32)
    cache_b_logf = jax.nn.log_sigmoid(F_BIAS_CENTER + nrm(ks[6], (DEPTH, n_pool, PAGE_SIZE, H_B), jnp.float32))
    page_table = jax.random.permutation(ks[7], n_pool)[:n_used].reshape(DEC_BATCH, n_pages).astype(jnp.int32)
    meta_tokens = nrm(ks[8], (N_META, D_MODEL), jnp.float32)
    norm1 = 1.0 + 0.02 * nrm(ks[9], (DEPTH, D_MODEL), jnp.float32)
    w_in = nrm(ks[10], (DEPTH, D_MODEL, PROJ_W), jnp.float32) * D_MODEL ** -0.5
    b_f = F_BIAS_CENTER + 0.5 * nrm(ks[11], (DEPTH, H_B), jnp.float32)
    qn_a = 1.0 + 0.02 * nrm(ks[12], (DEPTH, HEAD_DIM), jnp.float32)
    kn_a = 1.0 + 0.02 * nrm(ks[13], (DEPTH, HEAD_DIM), jnp.float32)
    qn_b = 1.0 + 0.02 * nrm(ks[14], (DEPTH, HEAD_DIM), jnp.float32)
    kn_b = 1.0 + 0.02 * nrm(ks[15], (DEPTH, HEAD_DIM), jnp.float32)
    lam_q1 = 0.1 * nrm(ks[16], (DEPTH, HEAD_DIM), jnp.float32)
    lam_k1 = 0.1 * nrm(ks[17], (DEPTH, HEAD_DIM), jnp.float32)
    lam_q2 = 0.1 * nrm(ks[18], (DEPTH, HEAD_DIM), jnp.float32)
    lam_k2 = 0.1 * nrm(ks[19], (DEPTH, HEAD_DIM), jnp.float32)
    subln = 1.0 + 0.02 * nrm(ks[20], (DEPTH, 2 * HEAD_DIM), jnp.float32)
    w_o = nrm(ks[21], (DEPTH, W_MIX, D_MODEL), jnp.float32) * W_MIX ** -0.5
    norm2 = 1.0 + 0.02 * nrm(ks[22], (DEPTH, D_MODEL), jnp.float32)
    w_up = nrm(ks[23], (DEPTH, D_MODEL, D_FF), jnp.float32) * D_MODEL ** -0.5
    w_down = nrm(ks[24], (DEPTH, D_FF, D_MODEL), jnp.float32) * D_FF ** -0.5
    return {"x_prompt": x_prompt, "x_sample": x_sample,
            "cache_a_k": cache_a_k, "cache_a_v": cache_a_v,
            "cache_b_k": cache_b_k, "cache_b_v": cache_b_v, "cache_b_logf": cache_b_logf,
            "page_table": page_table, "meta_tokens": meta_tokens, "norm1": norm1, "w_in": w_in,
            "b_f": b_f, "qn_a": qn_a, "kn_a": kn_a, "qn_b": qn_b, "kn_b": kn_b,
            "lam_q1": lam_q1, "lam_k1": lam_k1, "lam_q2": lam_q2, "lam_k2": lam_k2,
            "subln": subln, "w_o": w_o, "norm2": norm2, "w_up": w_up, "w_down": w_down}


def reference(x_prompt, x_sample, cache_a_k, cache_a_v, cache_b_k, cache_b_v, cache_b_logf, page_table,
              meta_tokens, norm1, w_in, b_f, qn_a, kn_a, qn_b, kn_b, lam_q1, lam_k1, lam_q2, lam_k2,
              subln, w_o, norm2, w_up, w_down):
    B = x_prompt.shape[0]
    DB = x_sample.shape[0]
    lams = [diff_lambda(lam_q1[l], lam_k1[l], lam_q2[l], lam_k2[l], l) for l in range(DEPTH)]

    meta = jnp.broadcast_to(meta_tokens.astype(x_prompt.dtype)[None], (B, N_META, D_MODEL))
    xp = jnp.concatenate([meta, x_prompt], axis=1)
    L = xp.shape[1]
    p_ak, p_av, p_bk, p_bv, p_bf = [], [], [], [], []
    for l in range(DEPTH):
        h = rmsnorm(xp, norm1[l])
        qa, ka, va, qb, kb, vb, logf = project(h, w_in[l], b_f[l], qn_a[l], kn_a[l], qn_b[l], kn_b[l])
        oa, ob = prompt_mix(qa, ka, va, qb, kb, vb, logf, lams[l])
        xp = finish(xp, oa, ob, l, subln[l], w_o[l], norm2[l], w_up[l], w_down[l])
        p_ak.append(ka.reshape(B, L, H_A, 2 * HEAD_DIM))
        p_av.append(va)
        p_bk.append(kb)
        p_bv.append(vb)
        p_bf.append(logf)
    y_prompt = xp[:, N_META:]

    xs = x_sample
    T = xs.shape[1]
    s_ak, s_av, s_bk, s_bv, s_bf = [], [], [], [], []
    for l in range(DEPTH):
        h = rmsnorm(xs, norm1[l])
        qa, ka, va, qb, kb, vb, logf = project(h, w_in[l], b_f[l], qn_a[l], kn_a[l], qn_b[l], kn_b[l])
        oa, ob = sample_mix(qa, ka, va, qb, kb, vb, logf, lams[l], l, cache_a_k, cache_a_v,
                            cache_b_k, cache_b_v, cache_b_logf, page_table)
        xs = finish(xs, oa, ob, l, subln[l], w_o[l], norm2[l], w_up[l], w_down[l])
        s_ak.append(ka.reshape(DB, T, H_A, 2 * HEAD_DIM))
        s_av.append(va)
        s_bk.append(kb)
        s_bv.append(vb)
        s_bf.append(logf)
    y_sample = xs

    new_a_k_prompt = jnp.stack(p_ak)
    new_a_v_prompt = jnp.stack(p_av)
    new_b_k_prompt = jnp.stack(p_bk)
    new_b_v_prompt = jnp.stack(p_bv)
    new_b_logf_prompt = jnp.stack(p_bf)
    new_a_k_sample = jnp.stack(s_ak)
    new_a_v_sample = jnp.stack(s_av)
    new_b_k_sample = jnp.stack(s_bk)
    new_b_v_sample = jnp.stack(s_bv)
    new_b_logf_sample = jnp.stack(s_bf)
    return (y_prompt, y_sample, new_a_k_prompt, new_a_v_prompt, new_b_k_prompt, new_b_v_prompt,
            new_b_logf_prompt, new_a_k_sample, new_a_v_sample, new_b_k_sample, new_b_v_sample,
            new_b_logf_sample)
```

```python
import functools
import math

import numpy as np
import jax
import jax.numpy as jnp
from jax import lax
from jax.experimental import pallas as pl
from jax.experimental.pallas import tpu as pltpu

F32 = jnp.float32
BF16 = jnp.bfloat16

HEAD_DIM = 64
N_META = 16
RMS_EPS = 1e-6
NEG_INF = -1e30
F_LANES = 128
LOG2E = math.log2(math.e)

LANE = 128
MXU_DIM = 256
BF16_SUBLANES = 16
VMEM_LIMIT = 56 * 1024 * 1024

SLOT = BF16_SUBLANES
N_PIECES = 3

ROW_TILE = 512
ATTN_TILE = 512
DECODE_PAGES = 4
FF_CHUNK = 1024


def _lambda_init(layer):
    return 0.8 - 0.6 * math.exp(-0.3 * layer)


def _alibi_slopes(h_a):
    return [2.0 ** (-8.0 * (i + 1) / h_a) for i in range(h_a)]


def _const_spec(shape):
    return pl.BlockSpec(shape, lambda *_: (0,) * len(shape))


def _params(n_axes):
    return pltpu.CompilerParams(dimension_semantics=("arbitrary",) * n_axes, vmem_limit_bytes=VMEM_LIMIT)


def _group_sum_matrix():
    idx = np.arange(MXU_DIM) // HEAD_DIM
    return jnp.asarray((idx[:, None] == idx[None, :]).astype(np.float32), BF16)


def _bias_placement(h_a, h_b):
    pm = np.zeros((N_PIECES * F_LANES, 4 * LANE), np.float32)
    ones = np.zeros((1, 4 * LANE), np.float32)
    for h in range(h_a):
        for p in range(N_PIECES):
            pm[p * F_LANES + h_b + h, 0 * LANE + SLOT * h + p] = 1.0
            pm[p * F_LANES + h_b + h, 2 * LANE + SLOT * h + N_PIECES + p] = -1.0
            ones[0, 0 * LANE + SLOT * h + N_PIECES + p] = 1.0
            ones[0, 2 * LANE + SLOT * h + p] = 1.0
    for h in range(h_b):
        for p in range(N_PIECES):
            pm[p * F_LANES + h, 1 * LANE + SLOT * h + p] = -1.0
            pm[p * F_LANES + h, 3 * LANE + SLOT * h + N_PIECES + p] = 1.0
            ones[0, 1 * LANE + SLOT * h + N_PIECES + p] = 1.0
            ones[0, 3 * LANE + SLOT * h + p] = 1.0
    return jnp.asarray(pm, BF16), jnp.asarray(ones, F32)


def _proj_kernel(x_ref, g1_ref, w_ref, wf_ref, bf_ref, sc_ref, gm_ref, pm_ref, ones_ref, avec_ref, c0_ref,
                 ka_ref, va_ref, kb_ref, vb_ref, lf_ref, cum_ref, k_ref, e_ref, qt_ref, vt_ref, ft_ref,
                 carry_ref, *, tiles_per_seq, pos0, w_a, w_b, h_b):
    i = pl.program_id(0)
    tr = x_ref.shape[0]
    seq_tile = i % tiles_per_seq

    x = x_ref[...]
    ms = jnp.mean(x * x, axis=-1, keepdims=True)
    h = (x * lax.rsqrt(ms + RMS_EPS) * g1_ref[...]).astype(BF16)

    def seg(off, width):
        return jnp.dot(h, w_ref[:, off:off + width], preferred_element_type=F32)

    def headnorm(p, scale_row):
        outs = []
        for c in range(0, p.shape[1], MXU_DIM):
            pc = p[:, c:c + MXU_DIM]
            ssq = jnp.dot((pc * pc).astype(BF16), gm_ref[...], preferred_element_type=F32)
            outs.append(pc * lax.rsqrt(ssq * (1.0 / HEAD_DIM) + RMS_EPS) * scale_row[:, c:c + MXU_DIM])
        return jnp.concatenate(outs, axis=1)

    qa = headnorm(seg(0, w_a), sc_ref[0:1, :])
    qt_ref[0:w_a, :] = qa.T.astype(BF16)
    ka = headnorm(seg(w_a, w_a), sc_ref[1:2, :])
    ka_ref[...] = ka
    k_ref[:, 0:w_a] = ka.astype(BF16)
    va = seg(2 * w_a, w_a)
    va_ref[...] = va
    vt_ref[0:w_a, :] = va.T.astype(BF16)
    off = 3 * w_a
    qb = headnorm(seg(off, w_b), sc_ref[2:3, :])
    qt_ref[w_a:w_a + w_b, :] = qb.T.astype(BF16)
    kb = headnorm(seg(off + w_b, w_b), sc_ref[3:4, :])
    kb_ref[...] = kb
    k_ref[:, w_a:w_a + w_b] = kb.astype(BF16)
    vb = seg(off + 2 * w_b, w_b)
    vb_ref[...] = vb
    vt_ref[w_a:w_a + w_b, :] = vb.T.astype(BF16)

    logit = jnp.dot(h, wf_ref[...], preferred_element_type=F32) + bf_ref[...]
    lf = jax.nn.log_sigmoid(logit)
    lf_ref[...] = lf

    @pl.when(seq_tile == 0)
    def _():
        carry_ref[...] = c0_ref[...]

    row = lax.broadcasted_iota(jnp.int32, lf.shape, 0)
    lane = lax.broadcasted_iota(jnp.int32, lf.shape, 1)
    c = lf
    k = 1
    while k < tr:
        c = c + jnp.where(row >= k, pltpu.roll(c, k, axis=0), 0.0)
        k *= 2
    c = c + carry_ref[...]
    carry_ref[...] = c[tr - 1:tr, :]
    cum_ref[...] = c

    pos = (row + (pos0 + seq_tile * tr)).astype(F32)
    z = jnp.where(lane < h_b, c * LOG2E, avec_ref[...] * pos)
    pieces = []
    for _ in range(N_PIECES):
        zp = z.astype(BF16)
        pieces.append(zp)
        z = z - zp.astype(F32)
    ef = jnp.dot(jnp.concatenate(pieces, axis=1), pm_ref[...], preferred_element_type=F32) + ones_ref[...]
    e_ref[...] = ef[:, :2 * LANE].astype(BF16)
    ft_ref[...] = ef[:, 2 * LANE:].T.astype(BF16)


def _proj(x, g1, w_main, w_f, b_f, scales, gmat, pmat, ones_row, avec, c0, *, tr, tiles_per_seq, pos0, w_a, w_b,
          h_b):
    r, d = x.shape
    n = r // tr
    wq = w_a + w_b
    sds = jax.ShapeDtypeStruct
    rows = lambda w: pl.BlockSpec((tr, w), lambda i: (i, 0))
    cols = lambda w: pl.BlockSpec((w, tr), lambda i: (0, i))
    kern = functools.partial(_proj_kernel, tiles_per_seq=tiles_per_seq, pos0=pos0, w_a=w_a, w_b=w_b, h_b=h_b)
    return pl.pallas_call(
        kern,
        grid=(n,),
        in_specs=[rows(d), _const_spec(g1.shape), _const_spec(w_main.shape), _const_spec(w_f.shape),
                  _const_spec(b_f.shape), _const_spec(scales.shape), _const_spec(gmat.shape),
                  _const_spec(pmat.shape), _const_spec(ones_row.shape), _const_spec(avec.shape),
                  _const_spec(c0.shape)],
        out_specs=[rows(w_a), rows(w_a), rows(w_b), rows(w_b), rows(F_LANES), rows(F_LANES),
                   rows(wq), rows(2 * LANE), cols(wq), cols(wq), cols(2 * LANE)],
        out_shape=[sds((r, w_a), F32), sds((r, w_a), F32), sds((r, w_b), F32), sds((r, w_b), F32),
                   sds((r, F_LANES), F32), sds((r, F_LANES), F32),
                   sds((r, wq), BF16), sds((r, 2 * LANE), BF16), sds((wq, r), BF16), sds((wq, r), BF16),
                   sds((2 * LANE, r), BF16)],
        scratch_shapes=[pltpu.VMEM((1, F_LANES), F32)],
        compiler_params=_params(1),
        name="proj",
    )(x, g1, w_main, w_f, b_f, scales, gmat, pmat, ones_row, avec, c0)


def _head_maps(h_a, h_b):
    w_a = h_a * 2 * HEAD_DIM
    maps = []
    for r in range(2 * h_a):
        h, m = divmod(r, 2)
        maps.append(dict(q_row=r * HEAD_DIM, half=m, k_col=h * LANE, e_col=0, f_row=SLOT * h,
                         v_row=h * LANE, v_dim=2 * HEAD_DIM, group=0, acc_row=r * LANE))
    for h in range(h_b):
        maps.append(dict(q_row=w_a + h * HEAD_DIM, half=h % 2, k_col=w_a + (h // 2) * LANE, e_col=LANE,
                         f_row=LANE + SLOT * h, v_row=w_a + h * HEAD_DIM, v_dim=HEAD_DIM, group=1,
                         acc_row=h * HEAD_DIM))
    return maps


def _attn_init(qt_ref, ft_ref, qs_ref, m_ref, l_ref, acca_ref, accb_ref, maps):
    m_ref[...] = jnp.full(m_ref.shape, NEG_INF, F32)
    l_ref[...] = jnp.zeros(l_ref.shape, F32)
    acca_ref[...] = jnp.zeros(acca_ref.shape, F32)
    accb_ref[...] = jnp.zeros(accb_ref.shape, F32)
    qs_ref[...] = jnp.zeros(qs_ref.shape, qs_ref.dtype)
    for r, mp in enumerate(maps):
        base = r * 2 * LANE
        q_dst = base + mp["half"] * HEAD_DIM
        qs_ref[q_dst:q_dst + HEAD_DIM, :] = qt_ref[mp["q_row"]:mp["q_row"] + HEAD_DIM, :]
        f_dst = base + LANE + mp["f_row"] % LANE
        qs_ref[f_dst:f_dst + SLOT, :] = ft_ref[mp["f_row"]:mp["f_row"] + SLOT, :]


def _attn_tile(k_ref, e_ref, vt_ref, mask, qs_ref, m_ref, l_ref, acca_ref, accb_ref, maps):
    for r, mp in enumerate(maps):
        kaug = jnp.concatenate([k_ref[:, mp["k_col"]:mp["k_col"] + LANE],
                                e_ref[:, mp["e_col"]:mp["e_col"] + LANE]], axis=1)
        s = jnp.dot(kaug, qs_ref[r * 2 * LANE:(r + 1) * 2 * LANE, :], preferred_element_type=F32)
        if mask is not None:
            s = jnp.where(mask, s, NEG_INF)
        m_old = m_ref[r:r + 1, :]
        m_new = jnp.maximum(m_old, jnp.max(s, axis=0, keepdims=True))
        alpha = jnp.exp2(m_old - m_new)
        p = jnp.exp2(s - m_new)
        l_ref[r:r + 1, :] = alpha * l_ref[r:r + 1, :] + jnp.sum(p, axis=0, keepdims=True)
        m_ref[r:r + 1, :] = m_new
        acc_ref = accb_ref if mp["group"] else acca_ref
        a0, d = mp["acc_row"], mp["v_dim"]
        pv = jnp.dot(vt_ref[mp["v_row"]:mp["v_row"] + d, :], p.astype(BF16), preferred_element_type=F32)
        acc_ref[a0:a0 + d, :] = alpha * acc_ref[a0:a0 + d, :] + pv


def _diff_lambda(lq1_ref, lk1_ref, lq2_ref, lk2_ref, lam_init):
    return (jnp.exp(jnp.sum(lq1_ref[...] * lk1_ref[...], axis=1, keepdims=True))
            - jnp.exp(jnp.sum(lq2_ref[...] * lk2_ref[...], axis=1, keepdims=True)) + lam_init)


def _attn_finalize(o_ref, osc_ref, lam, l_ref, acca_ref, accb_ref, h_a, h_b):
    inv_l = 1.0 / l_ref[...]
    outs = []
    for h in range(h_a):
        r1, r2 = 2 * h, 2 * h + 1
        o = (acca_ref[r1 * LANE:(r1 + 1) * LANE, :] * inv_l[r1:r1 + 1, :]
             - lam * (acca_ref[r2 * LANE:(r2 + 1) * LANE, :] * inv_l[r2:r2 + 1, :]))
        ms = jnp.mean(o * o, axis=0, keepdims=True)
        outs.append(o * lax.rsqrt(ms + RMS_EPS))
    for h in range(h_b):
        r = 2 * h_a + h
        outs.append(accb_ref[h * HEAD_DIM:(h + 1) * HEAD_DIM, :] * inv_l[r:r + 1, :])
    ot = jnp.concatenate(outs, axis=0)
    o_ref[...] = (ot.T * osc_ref[...]).astype(o_ref.dtype)


def _attn_main_kernel(qt_ref, ft_ref, k_ref, e_ref, vt_ref, km_ref, em_ref, vtm_ref,
                      lq1_ref, lk1_ref, lq2_ref, lk2_ref, osc_ref, o_ref,
                      qs_ref, m_ref, l_ref, acca_ref, accb_ref, *, h_a, h_b, lam_init):
    qi = pl.program_id(1)
    ki = pl.program_id(2)
    maps = _head_maps(h_a, h_b)
    state = (qs_ref, m_ref, l_ref, acca_ref, accb_ref)
    tk, tq = k_ref.shape[0], qt_ref.shape[1]

    @pl.when(ki == 0)
    def _():
        _attn_init(qt_ref, ft_ref, *state, maps)
        key = lax.broadcasted_iota(jnp.int32, (km_ref.shape[0], tq), 0)
        _attn_tile(km_ref, em_ref, vtm_ref, key < N_META, *state, maps)

    @pl.when(ki < qi)
    def _():
        _attn_tile(k_ref, e_ref, vt_ref, None, *state, maps)

    @pl.when(ki == qi)
    def _():
        key = lax.broadcasted_iota(jnp.int32, (tk, tq), 0)
        qry = lax.broadcasted_iota(jnp.int32, (tk, tq), 1)
        _attn_tile(k_ref, e_ref, vt_ref, key <= qry, *state, maps)
        lam = _diff_lambda(lq1_ref, lk1_ref, lq2_ref, lk2_ref, lam_init)
        _attn_finalize(o_ref, osc_ref, lam, l_ref, acca_ref, accb_ref, h_a, h_b)


def _attn_meta_kernel(qt_ref, ft_ref, km_ref, em_ref, vtm_ref, lq1_ref, lk1_ref, lq2_ref, lk2_ref, osc_ref, o_ref,
                      qs_ref, m_ref, l_ref, acca_ref, accb_ref, *, h_a, h_b, lam_init):
    maps = _head_maps(h_a, h_b)
    state = (qs_ref, m_ref, l_ref, acca_ref, accb_ref)
    tk, tq = km_ref.shape[0], qt_ref.shape[1]
    _attn_init(qt_ref, ft_ref, *state, maps)
    key = lax.broadcasted_iota(jnp.int32, (tk, tq), 0)
    qry = lax.broadcasted_iota(jnp.int32, (tk, tq), 1)
    _attn_tile(km_ref, em_ref, vtm_ref, (key <= qry) & (key < N_META), *state, maps)
    lam = _diff_lambda(lq1_ref, lk1_ref, lq2_ref, lk2_ref, lam_init)
    _attn_finalize(o_ref, osc_ref, lam, l_ref, acca_ref, accb_ref, h_a, h_b)


def _attn_scratch(tq, h_a, h_b):
    n_maps = 2 * h_a + h_b
    return [pltpu.VMEM((n_maps * 2 * LANE, tq), BF16), pltpu.VMEM((n_maps, tq), F32), pltpu.VMEM((n_maps, tq), F32),
            pltpu.VMEM((2 * h_a * LANE, tq), F32), pltpu.VMEM((h_b * HEAD_DIM, tq), F32)]


def _attn_main(qt, ft, k, e, vt, km, em, vtm, lams, osc, *, batch, tile, h_a, h_b, lam_init):
    wq, r = qt.shape
    n = r // batch // tile
    kv_blk = lambda b, qi, ki: b * n + jnp.minimum(ki, qi)
    in_specs = [
        pl.BlockSpec((wq, tile), lambda b, qi, ki: (0, b * n + qi)),
        pl.BlockSpec((2 * LANE, tile), lambda b, qi, ki: (0, b * n + qi)),
        pl.BlockSpec((tile, wq), lambda b, qi, ki: (kv_blk(b, qi, ki), 0)),
        pl.BlockSpec((tile, 2 * LANE), lambda b, qi, ki: (kv_blk(b, qi, ki), 0)),
        pl.BlockSpec((wq, tile), lambda b, qi, ki: (0, kv_blk(b, qi, ki))),
        _const_spec(km.shape), _const_spec(em.shape), _const_spec(vtm.shape),
    ] + [_const_spec(a.shape) for a in lams] + [_const_spec(osc.shape)]
    kern = functools.partial(_attn_main_kernel, h_a=h_a, h_b=h_b, lam_init=lam_init)
    return pl.pallas_call(
        kern,
        grid=(batch, n, n),
        in_specs=in_specs,
        out_specs=pl.BlockSpec((tile, wq), lambda b, qi, ki: (b * n + qi, 0)),
        out_shape=jax.ShapeDtypeStruct((r, wq), BF16),
        scratch_shapes=_attn_scratch(tile, h_a, h_b),
        compiler_params=_params(3),
        name="attn_main",
    )(qt, ft, k, e, vt, km, em, vtm, *lams, osc)


def _attn_meta(qt, ft, km, em, vtm, lams, osc, *, h_a, h_b, lam_init):
    wq, r = qt.shape
    args = (qt, ft, km, em, vtm, *lams, osc)
    kern = functools.partial(_attn_meta_kernel, h_a=h_a, h_b=h_b, lam_init=lam_init)
    return pl.pallas_call(
        kern,
        grid=(1,),
        in_specs=[_const_spec(a.shape) for a in args],
        out_specs=_const_spec((r, wq)),
        out_shape=jax.ShapeDtypeStruct((r, wq), BF16),
        scratch_shapes=_attn_scratch(r, h_a, h_b),
        compiler_params=_params(1),
        name="attn_meta",
    )(*args)


def _decode_kernel(pt_ref, qa_ref, qb_ref, kna_ref, knb_ref, vna_ref, vnb_ref, lfn_ref,
                   lq1_ref, lk1_ref, lq2_ref, lk2_ref, sub_ref, *rest,
                   g_pages, n_pages, page, h_a, h_b, lam_init):
    caches = rest[:5 * g_pages]
    oa_ref, ob_ref = rest[5 * g_pages:5 * g_pages + 2]
    qas_ref, qbs_ref, ma_ref, la_ref, mb_ref, lb_ref, acca_ref, accb_ref, carry_ref = rest[5 * g_pages + 2:]
    ak = [caches[5 * g + 0] for g in range(g_pages)]
    av = [caches[5 * g + 1] for g in range(g_pages)]
    bk = [caches[5 * g + 2] for g in range(g_pages)]
    bv = [caches[5 * g + 3] for g in range(g_pages)]
    lfp = [caches[5 * g + 4] for g in range(g_pages)]
    s_id = pl.program_id(1)
    n_steps = n_pages // g_pages
    n_rows = 2 * h_a
    past = n_pages * page
    v_a = 2 * HEAD_DIM

    row = lax.broadcasted_iota(jnp.int32, (n_rows, v_a), 0)
    lane = lax.broadcasted_iota(jnp.int32, (n_rows, v_a), 1)
    row_b = lax.broadcasted_iota(jnp.int32, (h_b, HEAD_DIM), 0)
    bcast = lambda col: jnp.broadcast_to(col, (col.shape[0], LANE))

    @pl.when(s_id == 0)
    def _():
        s_new_a = jnp.zeros((n_rows, 1), F32)
        for h in range(h_a):
            sel = ((row == 2 * h) & (lane < HEAD_DIM)) | ((row == 2 * h + 1) & (lane >= HEAD_DIM))
            qh = jnp.where(sel, qa_ref[h:h + 1, :], 0.0)
            qas_ref[h] = qh
            s_new_a = s_new_a + jnp.sum(qh * kna_ref[h:h + 1, :], axis=1, keepdims=True)
            acca_ref[h] = jnp.broadcast_to(vna_ref[h:h + 1, :], (n_rows, v_a))
        qb = qb_ref[...]
        for h in range(h_b):
            qbs_ref[h] = jnp.where(row_b == h, qb, 0.0)
            accb_ref[h] = jnp.broadcast_to(vnb_ref[h:h + 1, :], (h_b, HEAD_DIM))
        s_new_b = jnp.sum(qb * knb_ref[...], axis=1, keepdims=True)
        ma_ref[...] = bcast(s_new_a)
        mb_ref[...] = bcast(s_new_b)
        la_ref[...] = jnp.ones(la_ref.shape, F32)
        lb_ref[...] = jnp.ones(lb_ref.shape, F32)
        carry_ref[...] = bcast(lfn_ref[...])

    slope = jnp.zeros((n_rows, LANE), F32)
    row_l = lax.broadcasted_iota(jnp.int32, (n_rows, LANE), 0)
    lane_l = lax.broadcasted_iota(jnp.int32, (n_rows, LANE), 1)
    for h, sl in enumerate(_alibi_slopes(h_a)):
        slope = jnp.where(row_l // 2 == h, sl * LOG2E, slope)

    contract_last = (((1,), (1,)), ((), ()))
    sa_parts, sb_parts = [], []
    carry = carry_ref[...]
    for g in range(g_pages):
        pidx = n_pages - 1 - (s_id * g_pages + g)
        sa = jnp.zeros((n_rows, page), F32)
        for h in range(h_a):
            sa = sa + lax.dot_general(qas_ref[h].astype(BF16), ak[g][:, h, :].astype(BF16), contract_last,
                                      preferred_element_type=F32)
        dist = (past - pidx * page - lane_l).astype(F32)
        sa_parts.append(sa - slope * dist)
        sb = jnp.zeros((h_b, page), F32)
        for h in range(h_b):
            sb = sb + lax.dot_general(qbs_ref[h].astype(BF16), bk[g][:, h, :].astype(BF16), contract_last,
                                      preferred_element_type=F32)
        lt = lfp[g][...].T
        x = lt
        k = 1
        while k < page:
            x = x + jnp.where(lane_l + k < page, pltpu.roll(x, page - k, axis=1), 0.0)
            k *= 2
        sb_parts.append(sb + (carry + x - lt) * LOG2E)
        carry = carry + bcast(x[:, 0:1])
    carry_ref[...] = carry

    def update(s_parts, m_ref, l_ref):
        s = jnp.concatenate(s_parts, axis=1)
        m_old = m_ref[...][:, 0:1]
        m_new = jnp.maximum(m_old, jnp.max(s, axis=1, keepdims=True))
        alpha = jnp.exp2(m_old - m_new)
        p = jnp.exp2(s - m_new)
        l_ref[...] = bcast(alpha * l_ref[...][:, 0:1] + jnp.sum(p, axis=1, keepdims=True))
        m_ref[...] = bcast(m_new)
        return alpha, p.astype(BF16)

    alpha_a, pa = update(sa_parts, ma_ref, la_ref)
    for h in range(h_a):
        v = jnp.concatenate([av[g][:, h, :] for g in range(g_pages)], axis=0).astype(BF16)
        acca_ref[h] = alpha_a * acca_ref[h] + jnp.dot(pa, v, preferred_element_type=F32)
    alpha_b, pb = update(sb_parts, mb_ref, lb_ref)
    for h in range(h_b):
        v = jnp.concatenate([bv[g][:, h, :] for g in range(g_pages)], axis=0).astype(BF16)
        accb_ref[h] = alpha_b * accb_ref[h] + jnp.dot(pb, v, preferred_element_type=F32)

    @pl.when(s_id == n_steps - 1)
    def _():
        lam = _diff_lambda(lq1_ref, lk1_ref, lq2_ref, lk2_ref, lam_init)
        inv_a = 1.0 / la_ref[...][:, 0:1]
        rcol = row[:, 0:1]
        for h in range(h_a):
            coef = jnp.where(rcol == 2 * h, inv_a, jnp.where(rcol == 2 * h + 1, -lam * inv_a, 0.0))
            o = jnp.sum(coef * acca_ref[h], axis=0, keepdims=True)
            ms = jnp.mean(o * o, axis=1, keepdims=True)
            oa_ref[h:h + 1, :] = o * lax.rsqrt(ms + RMS_EPS) * sub_ref[...]
        inv_b = 1.0 / lb_ref[...][:, 0:1]
        ob = jnp.zeros((h_b, HEAD_DIM), F32)
        for h in range(h_b):
            ob = ob + jnp.where(row_b == h, accb_ref[h] * inv_b, 0.0)
        ob_ref[...] = ob


def _decode(page_table, qa, qb, kna, knb, vna, vnb, lfn, lams, sub, cache_a_k, cache_a_v, cache_b_k, cache_b_v,
            cache_b_logf, *, layer, lam_init):
    db, h_a, v_a = qa.shape
    h_b = qb.shape[1]
    n_pages = page_table.shape[1]
    page = cache_a_k.shape[2]
    g_pages = DECODE_PAGES
    n_steps = n_pages // g_pages

    def samp(a):
        return pl.BlockSpec((None,) + a.shape[1:], lambda b, s, pt: (b,) + (0,) * (a.ndim - 1))

    def paged(a, g):
        nd = a.ndim - 2
        return pl.BlockSpec((None, None) + a.shape[2:],
                            lambda b, s, pt: (layer, pt[b, n_pages - 1 - (s * g_pages + g)]) + (0,) * nd)

    small = (qa, qb, kna, knb, vna, vnb, lfn)
    consts = (*lams, sub)
    cache_args, cache_specs = [], []
    for g in range(g_pages):
        for a in (cache_a_k, cache_a_v, cache_b_k, cache_b_v, cache_b_logf):
            cache_args.append(a)
            cache_specs.append(paged(a, g))
    n_rows = 2 * h_a
    kern = functools.partial(_decode_kernel, g_pages=g_pages, n_pages=n_pages, page=page, h_a=h_a, h_b=h_b,
                             lam_init=lam_init)
    grid_spec = pltpu.PrefetchScalarGridSpec(
        num_scalar_prefetch=1,
        grid=(db, n_steps),
        in_specs=[samp(a) for a in small] + [_const_spec(a.shape) for a in consts] + cache_specs,
        out_specs=[pl.BlockSpec((None, h_a, v_a), lambda b, s, pt: (b, 0, 0)),
                   pl.BlockSpec((None, h_b, HEAD_DIM), lambda b, s, pt: (b, 0, 0))],
        scratch_shapes=[pltpu.VMEM((h_a, n_rows, v_a), F32), pltpu.VMEM((h_b, h_b, HEAD_DIM), F32),
                        pltpu.VMEM((n_rows, LANE), F32), pltpu.VMEM((n_rows, LANE), F32),
                        pltpu.VMEM((h_b, LANE), F32), pltpu.VMEM((h_b, LANE), F32),
                        pltpu.VMEM((h_a, n_rows, v_a), F32), pltpu.VMEM((h_b, h_b, HEAD_DIM), F32),
                        pltpu.VMEM((h_b, LANE), F32)],
    )
    return pl.pallas_call(
        kern,
        grid_spec=grid_spec,
        out_shape=[jax.ShapeDtypeStruct((db, h_a, v_a), F32), jax.ShapeDtypeStruct((db, h_b, HEAD_DIM), F32)],
        compiler_params=_params(2),
        name="decode",
    )(page_table, *small, *consts, *cache_args)


def _finish_kernel(x_ref, o_ref, wo_ref, g2_ref, wup_ref, wdn_ref, y_ref):
    x1 = x_ref[...] + jnp.dot(o_ref[...], wo_ref[...], preferred_element_type=F32)
    ms = jnp.mean(x1 * x1, axis=-1, keepdims=True)
    hn = (x1 * lax.rsqrt(ms + RMS_EPS) * g2_ref[...]).astype(BF16)
    acc = x1
    for c in range(0, wup_ref.shape[1], FF_CHUNK):
        u = jnp.maximum(jnp.dot(hn, wup_ref[:, c:c + FF_CHUNK], preferred_element_type=F32), 0.0)
        acc = acc + jnp.dot((u * u).astype(BF16), wdn_ref[c:c + FF_CHUNK, :], preferred_element_type=F32)
    y_ref[...] = acc


def _finish(x, o, w_o, g2, w_up, w_down, *, tr):
    r, d = x.shape
    rows = lambda w: pl.BlockSpec((tr, w), lambda i: (i, 0))
    return pl.pallas_call(
        _finish_kernel,
        grid=(r // tr,),
        in_specs=[rows(d), rows(o.shape[1]), _const_spec(w_o.shape), _const_spec(g2.shape),
                  _const_spec(w_up.shape), _const_spec(w_down.shape)],
        out_specs=rows(d),
        out_shape=jax.ShapeDtypeStruct((r, d), F32),
        compiler_params=_params(1),
        name="finish",
    )(x, o, w_o, g2, w_up, w_down)


def kernel(x_prompt, x_sample, cache_a_k, cache_a_v, cache_b_k, cache_b_v, cache_b_logf, page_table, meta_tokens,
           norm1, w_in, b_f, qn_a, kn_a, qn_b, kn_b, lam_q1, lam_k1, lam_q2, lam_k2, subln, w_o, norm2, w_up,
           w_down):
    batch, seq, d = x_prompt.shape
    db, dec_seq, _ = x_sample.shape
    depth, _, page, h_a, v_a = cache_a_k.shape
    h_b = cache_b_k.shape[3]
    w_a, w_b = h_a * v_a, h_b * HEAD_DIM
    wq = w_a + w_b
    n_small = LANE
    assert dec_seq == 1 and v_a == 2 * HEAD_DIM and meta_tokens.shape[0] == N_META
    assert w_a == w_b and w_a % MXU_DIM == 0 and h_a + h_b <= F_LANES
    assert SLOT * h_a <= LANE and SLOT * h_b <= LANE and N_META + db <= n_small
    assert seq % ATTN_TILE == 0 and seq % ROW_TILE == 0 and page_table.shape[1] % DECODE_PAGES == 0
    assert w_in.shape[2] == 3 * wq + h_b

    gmat = _group_sum_matrix()
    pmat, ones_row = _bias_placement(h_a, h_b)
    avec = np.zeros((1, F_LANES), np.float32)
    avec[0, h_b:h_b + h_a] = [s * LOG2E for s in _alibi_slopes(h_a)]
    avec = jnp.asarray(avec)
    q_scale = HEAD_DIM ** -0.5 * LOG2E

    x_main = x_prompt.reshape(batch * seq, d)
    x_small = jnp.concatenate([meta_tokens.astype(F32), x_sample.reshape(db, d),
                               jnp.zeros((n_small - N_META - db, d), F32)], axis=0)
    samples = slice(N_META, N_META + db)
    zero_c0 = jnp.zeros((1, F_LANES), F32)

    outs = {name: [] for name in ("p_ak", "p_av", "p_bk", "p_bv", "p_lf", "s_ak", "s_av", "s_bk", "s_bv", "s_lf")}
    for l in range(depth):
        lam_init = _lambda_init(l)
        w_main = w_in[l, :, :3 * wq].astype(BF16)
        w_f = jnp.pad(w_in[l, :, 3 * wq:], ((0, 0), (0, F_LANES - h_b))).astype(BF16)
        bias_f = jnp.pad(b_f[l], (0, F_LANES - h_b)).reshape(1, F_LANES)
        g1 = norm1[l].reshape(1, d)
        scales = jnp.stack([jnp.tile(qn_a[l], w_a // HEAD_DIM) * q_scale, jnp.tile(kn_a[l], w_a // HEAD_DIM),
                            jnp.tile(qn_b[l], w_b // HEAD_DIM) * q_scale, jnp.tile(kn_b[l], w_b // HEAD_DIM)])
        lams = tuple(a[l].reshape(1, HEAD_DIM) for a in (lam_q1, lam_k1, lam_q2, lam_k2))
        sub_row = (subln[l] * (1.0 - lam_init)).reshape(1, v_a)
        osc = jnp.concatenate([jnp.tile(sub_row, (1, h_a)), jnp.ones((1, w_b), F32)], axis=1)
        proj = functools.partial(_proj, g1=g1, w_main=w_main, w_f=w_f, b_f=bias_f, scales=scales, gmat=gmat,
                                 pmat=pmat, ones_row=ones_row, avec=avec, w_a=w_a, w_b=w_b, h_b=h_b)

        (ka_s, va_s, kb_s, vb_s, lf_s, cum_s, k_s, e_s, qt_s, vt_s, ft_s) = proj(
            x_small, c0=zero_c0, tr=n_small, tiles_per_seq=1, pos0=0)
        (ka_m, va_m, kb_m, vb_m, lf_m, _, k_m, e_m, qt_m, vt_m, ft_m) = proj(
            x_main, c0=cum_s[N_META - 1:N_META], tr=ROW_TILE, tiles_per_seq=seq // ROW_TILE, pos0=N_META)

        o_meta = _attn_meta(qt_s, ft_s, k_s, e_s, vt_s, lams, osc, h_a=h_a, h_b=h_b, lam_init=lam_init)
        q_rows = qt_s[:, samples].T.astype(F32)
        oa_s, ob_s = _decode(
            page_table, q_rows[:, :w_a].reshape(db, h_a, v_a), q_rows[:, w_a:].reshape(db, h_b, HEAD_DIM),
            ka_s[samples].reshape(db, h_a, v_a), kb_s[samples].reshape(db, h_b, HEAD_DIM),
            va_s[samples].reshape(db, h_a, v_a), vb_s[samples].reshape(db, h_b, HEAD_DIM),
            lf_s[samples, :h_b].reshape(db, h_b, 1), lams, sub_row,
            cache_a_k, cache_a_v, cache_b_k, cache_b_v, cache_b_logf, layer=l, lam_init=lam_init)
        o_main = _attn_main(qt_m, ft_m, k_m, e_m, vt_m, k_s, e_s, vt_s, lams, osc, batch=batch, tile=ATTN_TILE,
                            h_a=h_a, h_b=h_b, lam_init=lam_init)
        o_samples = jnp.concatenate([oa_s.reshape(db, w_a), ob_s.reshape(db, w_b)], axis=1).astype(BF16)
        o_small = jnp.concatenate([o_meta[:N_META], o_samples, jnp.zeros((n_small - N_META - db, wq), BF16)],
                                  axis=0)

        fin = functools.partial(_finish, w_o=w_o[l].astype(BF16), g2=norm2[l].reshape(1, d),
                                w_up=w_up[l].astype(BF16), w_down=w_down[l].astype(BF16))
        x_small = fin(x_small, o_small, tr=n_small)
        x_main = fin(x_main, o_main, tr=ROW_TILE)

        def prompt_rows(small, main, tail):
            meta = jnp.broadcast_to(small[:N_META][None], (batch, N_META, small.shape[1]))
            full = jnp.concatenate([meta, main.reshape(batch, seq, main.shape[1])], axis=1)
            return full.reshape((batch, N_META + seq) + tail)

        outs["p_ak"].append(prompt_rows(ka_s, ka_m, (h_a, v_a)))
        outs["p_av"].append(prompt_rows(va_s, va_m, (h_a, v_a)))
        outs["p_bk"].append(prompt_rows(kb_s, kb_m, (h_b, HEAD_DIM)))
        outs["p_bv"].append(prompt_rows(vb_s, vb_m, (h_b, HEAD_DIM)))
        outs["p_lf"].append(prompt_rows(lf_s[:, :h_b], lf_m[:, :h_b], (h_b,)))
        outs["s_ak"].append(ka_s[samples].reshape(db, 1, h_a, v_a))
        outs["s_av"].append(va_s[samples].reshape(db, 1, h_a, v_a))
        outs["s_bk"].append(kb_s[samples].reshape(db, 1, h_b, HEAD_DIM))
        outs["s_bv"].append(vb_s[samples].reshape(db, 1, h_b, HEAD_DIM))
        outs["s_lf"].append(lf_s[samples, :h_b].reshape(db, 1, h_b))

    y_prompt = x_main.reshape(batch, seq, d)
    y_sample = x_small[samples].reshape(db, 1, d)
    stack = lambda name: jnp.stack(outs[name])
    return (y_prompt, y_sample, stack("p_ak"), stack("p_av"), stack("p_bk"), stack("p_bv"), stack("p_lf"),
            stack("s_ak"), stack("s_av"), stack("s_bk"), stack("s_bv"), stack("s_lf"))
```

```python
import functools
import math

import numpy as np
import jax
import jax.numpy as jnp
from jax import lax
from jax.experimental import pallas as pl
from jax.experimental.pallas import tpu as pltpu

F32 = jnp.float32
BF16 = jnp.bfloat16

HEAD_DIM = 64
N_META = 16
RMS_EPS = 1e-6
NEG_INF = -1e30
F_LANES = 128
LOG2E = math.log2(math.e)

LANE = 128
MXU_DIM = 256
BF16_SUBLANES = 16
VMEM_LIMIT = 56 * 1024 * 1024

SLOT = BF16_SUBLANES
N_PIECES = 3

ROW_TILE = 512
ATTN_TILE = 512
DECODE_PAGES = 8
FF_CHUNK = 1024


def _lambda_init(layer):
    return 0.8 - 0.6 * math.exp(-0.3 * layer)


def _alibi_slopes(h_a):
    return [2.0 ** (-8.0 * (i + 1) / h_a) for i in range(h_a)]


def _const_spec(shape):
    return pl.BlockSpec(shape, lambda *_: (0,) * len(shape))


def _params(n_axes):
    return pltpu.CompilerParams(dimension_semantics=("arbitrary",) * n_axes, vmem_limit_bytes=VMEM_LIMIT)


def _group_sum_matrix():
    idx = np.arange(MXU_DIM) // HEAD_DIM
    return jnp.asarray((idx[:, None] == idx[None, :]).astype(np.float32), BF16)


def _bias_placement(h_a, h_b):
    pm = np.zeros((N_PIECES * F_LANES, 4 * LANE), np.float32)
    ones = np.zeros((1, 4 * LANE), np.float32)
    for h in range(h_a):
        for p in range(N_PIECES):
            pm[p * F_LANES + h_b + h, 0 * LANE + SLOT * h + p] = 1.0
            pm[p * F_LANES + h_b + h, 2 * LANE + SLOT * h + N_PIECES + p] = -1.0
            ones[0, 0 * LANE + SLOT * h + N_PIECES + p] = 1.0
            ones[0, 2 * LANE + SLOT * h + p] = 1.0
    for h in range(h_b):
        for p in range(N_PIECES):
            pm[p * F_LANES + h, 1 * LANE + SLOT * h + p] = -1.0
            pm[p * F_LANES + h, 3 * LANE + SLOT * h + N_PIECES + p] = 1.0
            ones[0, 1 * LANE + SLOT * h + N_PIECES + p] = 1.0
            ones[0, 3 * LANE + SLOT * h + p] = 1.0
    return jnp.asarray(pm, BF16), jnp.asarray(ones, F32)


def _proj_kernel(x_ref, g1_ref, w_ref, wf_ref, bf_ref, sc_ref, gm_ref, pm_ref, ones_ref, avec_ref, c0_ref,
                 ka_ref, va_ref, kb_ref, vb_ref, lf_ref, cum_ref, k_ref, e_ref, qt_ref, vt_ref, ft_ref,
                 carry_ref, *, tiles_per_seq, pos0, w_a, w_b, h_b):
    i = pl.program_id(0)
    tr = x_ref.shape[0]
    seq_tile = i % tiles_per_seq

    x = x_ref[...]
    ms = jnp.mean(x * x, axis=-1, keepdims=True)
    h = (x * lax.rsqrt(ms + RMS_EPS) * g1_ref[...]).astype(BF16)

    def seg(off, width):
        return jnp.dot(h, w_ref[:, off:off + width], preferred_element_type=F32)

    def headnorm(p, scale_row):
        outs = []
        for c in range(0, p.shape[1], MXU_DIM):
            pc = p[:, c:c + MXU_DIM]
            ssq = jnp.dot((pc * pc).astype(BF16), gm_ref[...], preferred_element_type=F32)
            outs.append(pc * lax.rsqrt(ssq * (1.0 / HEAD_DIM) + RMS_EPS) * scale_row[:, c:c + MXU_DIM])
        return jnp.concatenate(outs, axis=1)

    qa = headnorm(seg(0, w_a), sc_ref[0:1, :])
    qt_ref[0:w_a, :] = qa.T.astype(BF16)
    ka = headnorm(seg(w_a, w_a), sc_ref[1:2, :])
    ka_ref[...] = ka
    k_ref[:, 0:w_a] = ka.astype(BF16)
    va = seg(2 * w_a, w_a)
    va_ref[...] = va
    vt_ref[0:w_a, :] = va.T.astype(BF16)
    off = 3 * w_a
    qb = headnorm(seg(off, w_b), sc_ref[2:3, :])
    qt_ref[w_a:w_a + w_b, :] = qb.T.astype(BF16)
    kb = headnorm(seg(off + w_b, w_b), sc_ref[3:4, :])
    kb_ref[...] = kb
    k_ref[:, w_a:w_a + w_b] = kb.astype(BF16)
    vb = seg(off + 2 * w_b, w_b)
    vb_ref[...] = vb
    vt_ref[w_a:w_a + w_b, :] = vb.T.astype(BF16)

    logit = jnp.dot(h, wf_ref[...], preferred_element_type=F32) + bf_ref[...]
    lf = jax.nn.log_sigmoid(logit)
    lf_ref[...] = lf

    @pl.when(seq_tile == 0)
    def _():
        carry_ref[...] = c0_ref[...]

    row = lax.broadcasted_iota(jnp.int32, lf.shape, 0)
    lane = lax.broadcasted_iota(jnp.int32, lf.shape, 1)
    c = lf
    k = 1
    while k < tr:
        c = c + jnp.where(row >= k, pltpu.roll(c, k, axis=0), 0.0)
        k *= 2
    c = c + carry_ref[...]
    carry_ref[...] = c[tr - 1:tr, :]
    cum_ref[...] = c

    pos = (row + (pos0 + seq_tile * tr)).astype(F32)
    z = jnp.where(lane < h_b, c * LOG2E, avec_ref[...] * pos)
    pieces = []
    for _ in range(N_PIECES):
        zp = z.astype(BF16)
        pieces.append(zp)
        z = z - zp.astype(F32)
    ef = jnp.dot(jnp.concatenate(pieces, axis=1), pm_ref[...], preferred_element_type=F32) + ones_ref[...]
    e_ref[...] = ef[:, :2 * LANE].astype(BF16)
    ft_ref[...] = ef[:, 2 * LANE:].T.astype(BF16)


def _proj(x, g1, w_main, w_f, b_f, scales, gmat, pmat, ones_row, avec, c0, *, tr, tiles_per_seq, pos0, w_a, w_b,
          h_b):
    r, d = x.shape
    n = r // tr
    wq = w_a + w_b
    sds = jax.ShapeDtypeStruct
    rows = lambda w: pl.BlockSpec((tr, w), lambda i: (i, 0))
    cols = lambda w: pl.BlockSpec((w, tr), lambda i: (0, i))
    kern = functools.partial(_proj_kernel, tiles_per_seq=tiles_per_seq, pos0=pos0, w_a=w_a, w_b=w_b, h_b=h_b)
    return pl.pallas_call(
        kern,
        grid=(n,),
        in_specs=[rows(d), _const_spec(g1.shape), _const_spec(w_main.shape), _const_spec(w_f.shape),
                  _const_spec(b_f.shape), _const_spec(scales.shape), _const_spec(gmat.shape),
                  _const_spec(pmat.shape), _const_spec(ones_row.shape), _const_spec(avec.shape),
                  _const_spec(c0.shape)],
        out_specs=[rows(w_a), rows(w_a), rows(w_b), rows(w_b), rows(F_LANES), rows(F_LANES),
                   rows(wq), rows(2 * LANE), cols(wq), cols(wq), cols(2 * LANE)],
        out_shape=[sds((r, w_a), F32), sds((r, w_a), F32), sds((r, w_b), F32), sds((r, w_b), F32),
                   sds((r, F_LANES), F32), sds((r, F_LANES), F32),
                   sds((r, wq), BF16), sds((r, 2 * LANE), BF16), sds((wq, r), BF16), sds((wq, r), BF16),
                   sds((2 * LANE, r), BF16)],
        scratch_shapes=[pltpu.VMEM((1, F_LANES), F32)],
        compiler_params=_params(1),
        name="proj",
    )(x, g1, w_main, w_f, b_f, scales, gmat, pmat, ones_row, avec, c0)


def _head_maps(h_a, h_b):
    w_a = h_a * 2 * HEAD_DIM
    maps = []
    for r in range(2 * h_a):
        h, m = divmod(r, 2)
        maps.append(dict(q_row=r * HEAD_DIM, half=m, k_col=h * LANE, e_col=0, f_row=SLOT * h,
                         v_row=h * LANE, v_dim=2 * HEAD_DIM, group=0, acc_row=r * LANE))
    for h in range(h_b):
        maps.append(dict(q_row=w_a + h * HEAD_DIM, half=h % 2, k_col=w_a + (h // 2) * LANE, e_col=LANE,
                         f_row=LANE + SLOT * h, v_row=w_a + h * HEAD_DIM, v_dim=HEAD_DIM, group=1,
                         acc_row=h * HEAD_DIM))
    return maps


def _attn_init(qt_ref, ft_ref, qs_ref, m_ref, l_ref, acca_ref, accb_ref, maps):
    m_ref[...] = jnp.full(m_ref.shape, NEG_INF, F32)
    l_ref[...] = jnp.zeros(l_ref.shape, F32)
    acca_ref[...] = jnp.zeros(acca_ref.shape, F32)
    accb_ref[...] = jnp.zeros(accb_ref.shape, F32)
    qs_ref[...] = jnp.zeros(qs_ref.shape, qs_ref.dtype)
    for r, mp in enumerate(maps):
        base = r * 2 * LANE
        q_dst = base + mp["half"] * HEAD_DIM
        qs_ref[q_dst:q_dst + HEAD_DIM, :] = qt_ref[mp["q_row"]:mp["q_row"] + HEAD_DIM, :]
        f_dst = base + LANE + mp["f_row"] % LANE
        qs_ref[f_dst:f_dst + SLOT, :] = ft_ref[mp["f_row"]:mp["f_row"] + SLOT, :]


def _attn_tile(k_ref, e_ref, vt_ref, mask, qs_ref, m_ref, l_ref, acca_ref, accb_ref, maps):
    for r, mp in enumerate(maps):
        kaug = jnp.concatenate([k_ref[:, mp["k_col"]:mp["k_col"] + LANE],
                                e_ref[:, mp["e_col"]:mp["e_col"] + LANE]], axis=1)
        s = jnp.dot(kaug, qs_ref[r * 2 * LANE:(r + 1) * 2 * LANE, :], preferred_element_type=F32)
        if mask is not None:
            s = jnp.where(mask, s, NEG_INF)
        m_old = m_ref[r:r + 1, :]
        m_new = jnp.maximum(m_old, jnp.max(s, axis=0, keepdims=True))
        alpha = jnp.exp2(m_old - m_new)
        p = jnp.exp2(s - m_new)
        l_ref[r:r + 1, :] = alpha * l_ref[r:r + 1, :] + jnp.sum(p, axis=0, keepdims=True)
        m_ref[r:r + 1, :] = m_new
        acc_ref = accb_ref if mp["group"] else acca_ref
        a0, d = mp["acc_row"], mp["v_dim"]
        pv = jnp.dot(vt_ref[mp["v_row"]:mp["v_row"] + d, :], p.astype(BF16), preferred_element_type=F32)
        acc_ref[a0:a0 + d, :] = alpha * acc_ref[a0:a0 + d, :] + pv


def _diff_lambda(lq1_ref, lk1_ref, lq2_ref, lk2_ref, lam_init):
    return (jnp.exp(jnp.sum(lq1_ref[...] * lk1_ref[...], axis=1, keepdims=True))
            - jnp.exp(jnp.sum(lq2_ref[...] * lk2_ref[...], axis=1, keepdims=True)) + lam_init)


def _attn_finalize(o_ref, osc_ref, lam, l_ref, acca_ref, accb_ref, h_a, h_b):
    inv_l = 1.0 / l_ref[...]
    outs = []
    for h in range(h_a):
        r1, r2 = 2 * h, 2 * h + 1
        o = (acca_ref[r1 * LANE:(r1 + 1) * LANE, :] * inv_l[r1:r1 + 1, :]
             - lam * (acca_ref[r2 * LANE:(r2 + 1) * LANE, :] * inv_l[r2:r2 + 1, :]))
        ms = jnp.mean(o * o, axis=0, keepdims=True)
        outs.append(o * lax.rsqrt(ms + RMS_EPS))
    for h in range(h_b):
        r = 2 * h_a + h
        outs.append(accb_ref[h * HEAD_DIM:(h + 1) * HEAD_DIM, :] * inv_l[r:r + 1, :])
    ot = jnp.concatenate(outs, axis=0)
    o_ref[...] = (ot.T * osc_ref[...]).astype(o_ref.dtype)


def _attn_main_kernel(qt_ref, ft_ref, k_ref, e_ref, vt_ref, km_ref, em_ref, vtm_ref,
                      lq1_ref, lk1_ref, lq2_ref, lk2_ref, osc_ref, o_ref,
                      qs_ref, m_ref, l_ref, acca_ref, accb_ref, *, h_a, h_b, lam_init):
    qi = pl.program_id(1)
    ki = pl.program_id(2)
    maps = _head_maps(h_a, h_b)
    state = (qs_ref, m_ref, l_ref, acca_ref, accb_ref)
    tk, tq = k_ref.shape[0], qt_ref.shape[1]

    @pl.when(ki == 0)
    def _():
        _attn_init(qt_ref, ft_ref, *state, maps)
        key = lax.broadcasted_iota(jnp.int32, (km_ref.shape[0], tq), 0)
        _attn_tile(km_ref, em_ref, vtm_ref, key < N_META, *state, maps)

    @pl.when(ki < qi)
    def _():
        _attn_tile(k_ref, e_ref, vt_ref, None, *state, maps)

    @pl.when(ki == qi)
    def _():
        key = lax.broadcasted_iota(jnp.int32, (tk, tq), 0)
        qry = lax.broadcasted_iota(jnp.int32, (tk, tq), 1)
        _attn_tile(k_ref, e_ref, vt_ref, key <= qry, *state, maps)
        lam = _diff_lambda(lq1_ref, lk1_ref, lq2_ref, lk2_ref, lam_init)
        _attn_finalize(o_ref, osc_ref, lam, l_ref, acca_ref, accb_ref, h_a, h_b)


def _attn_meta_kernel(qt_ref, ft_ref, km_ref, em_ref, vtm_ref, lq1_ref, lk1_ref, lq2_ref, lk2_ref, osc_ref, o_ref,
                      qs_ref, m_ref, l_ref, acca_ref, accb_ref, *, h_a, h_b, lam_init):
    maps = _head_maps(h_a, h_b)
    state = (qs_ref, m_ref, l_ref, acca_ref, accb_ref)
    tk, tq = km_ref.shape[0], qt_ref.shape[1]
    _attn_init(qt_ref, ft_ref, *state, maps)
    key = lax.broadcasted_iota(jnp.int32, (tk, tq), 0)
    qry = lax.broadcasted_iota(jnp.int32, (tk, tq), 1)
    _attn_tile(km_ref, em_ref, vtm_ref, (key <= qry) & (key < N_META), *state, maps)
    lam = _diff_lambda(lq1_ref, lk1_ref, lq2_ref, lk2_ref, lam_init)
    _attn_finalize(o_ref, osc_ref, lam, l_ref, acca_ref, accb_ref, h_a, h_b)


def _attn_scratch(tq, h_a, h_b):
    n_maps = 2 * h_a + h_b
    return [pltpu.VMEM((n_maps * 2 * LANE, tq), BF16), pltpu.VMEM((n_maps, tq), F32), pltpu.VMEM((n_maps, tq), F32),
            pltpu.VMEM((2 * h_a * LANE, tq), F32), pltpu.VMEM((h_b * HEAD_DIM, tq), F32)]


def _attn_main(qt, ft, k, e, vt, km, em, vtm, lams, osc, *, batch, tile, h_a, h_b, lam_init):
    wq, r = qt.shape
    n = r // batch // tile
    kv_blk = lambda b, qi, ki: b * n + jnp.minimum(ki, qi)
    in_specs = [
        pl.BlockSpec((wq, tile), lambda b, qi, ki: (0, b * n + qi)),
        pl.BlockSpec((2 * LANE, tile), lambda b, qi, ki: (0, b * n + qi)),
        pl.BlockSpec((tile, wq), lambda b, qi, ki: (kv_blk(b, qi, ki), 0)),
        pl.BlockSpec((tile, 2 * LANE), lambda b, qi, ki: (kv_blk(b, qi, ki), 0)),
        pl.BlockSpec((wq, tile), lambda b, qi, ki: (0, kv_blk(b, qi, ki))),
        _const_spec(km.shape), _const_spec(em.shape), _const_spec(vtm.shape),
    ] + [_const_spec(a.shape) for a in lams] + [_const_spec(osc.shape)]
    kern = functools.partial(_attn_main_kernel, h_a=h_a, h_b=h_b, lam_init=lam_init)
    return pl.pallas_call(
        kern,
        grid=(batch, n, n),
        in_specs=in_specs,
        out_specs=pl.BlockSpec((tile, wq), lambda b, qi, ki: (b * n + qi, 0)),
        out_shape=jax.ShapeDtypeStruct((r, wq), BF16),
        scratch_shapes=_attn_scratch(tile, h_a, h_b),
        compiler_params=_params(3),
        name="attn_main",
    )(qt, ft, k, e, vt, km, em, vtm, *lams, osc)


def _attn_meta(qt, ft, km, em, vtm, lams, osc, *, h_a, h_b, lam_init):
    wq, r = qt.shape
    args = (qt, ft, km, em, vtm, *lams, osc)
    kern = functools.partial(_attn_meta_kernel, h_a=h_a, h_b=h_b, lam_init=lam_init)
    return pl.pallas_call(
        kern,
        grid=(1,),
        in_specs=[_const_spec(a.shape) for a in args],
        out_specs=_const_spec((r, wq)),
        out_shape=jax.ShapeDtypeStruct((r, wq), BF16),
        scratch_shapes=_attn_scratch(r, h_a, h_b),
        compiler_params=_params(1),
        name="attn_meta",
    )(*args)


def _decode_kernel(pt_ref, qa_ref, qb_ref, kna_ref, knb_ref, vna_ref, vnb_ref, lfn_ref,
                   lq1_ref, lk1_ref, lq2_ref, lk2_ref, sub_ref, *rest,
                   g_pages, n_pages, page, h_a, h_b, lam_init):
    caches = rest[:5 * g_pages]
    oa_ref, ob_ref = rest[5 * g_pages:5 * g_pages + 2]
    qas_ref, qbs_ref, ma_ref, la_ref, mb_ref, lb_ref, acca_ref, accb_ref, carry_ref = rest[5 * g_pages + 2:]
    ak, av, bk, bv, lfp = ([caches[5 * g + j] for g in range(g_pages)] for j in range(5))
    s_id = pl.program_id(1)
    n_steps = n_pages // g_pages
    n_rows = 2 * h_a
    past = n_pages * page
    v_a = 2 * HEAD_DIM
    w_b = h_b * HEAD_DIM
    a_keys = page * h_a

    bcast = lambda col: jnp.broadcast_to(col, (col.shape[0], LANE))
    row_a = lax.broadcasted_iota(jnp.int32, (n_rows, v_a), 0)
    lane_a = lax.broadcasted_iota(jnp.int32, (n_rows, v_a), 1)
    row_b = lax.broadcasted_iota(jnp.int32, (h_b, w_b), 0)
    own_b = lax.broadcasted_iota(jnp.int32, (h_b, w_b), 1) // HEAD_DIM == row_b

    def per_head_rows(ref):
        out = jnp.zeros((n_rows, v_a), F32)
        for h in range(h_a):
            out = jnp.where(row_a // 2 == h, ref[h:h + 1, :], out)
        return out

    @pl.when(s_id == 0)
    def _():
        half_sel = (lane_a // HEAD_DIM) == (row_a % 2)
        q8 = jnp.where(half_sel, per_head_rows(qa_ref), 0.0)
        qas_ref[...] = q8.astype(BF16)
        qblk = jnp.where(own_b, qb_ref[...], 0.0)
        qbs_ref[...] = qblk.astype(BF16)
        ma_ref[...] = bcast(jnp.sum(q8 * per_head_rows(kna_ref), axis=1, keepdims=True))
        mb_ref[...] = bcast(jnp.sum(qblk * knb_ref[...], axis=1, keepdims=True))
        la_ref[...] = jnp.ones(la_ref.shape, F32)
        lb_ref[...] = jnp.ones(lb_ref.shape, F32)
        acca_ref[...] = per_head_rows(vna_ref)
        accb_ref[...] = jnp.broadcast_to(vnb_ref[...], (h_b, w_b))
        carry_ref[...] = bcast(lfn_ref[...])

    row_s = lax.broadcasted_iota(jnp.int32, (n_rows, a_keys), 0)
    lane_s = lax.broadcasted_iota(jnp.int32, (n_rows, a_keys), 1)
    valid_a = (lane_s % h_a) == (row_s // 2)
    slope = jnp.zeros((n_rows, a_keys), F32)
    for h, sl in enumerate(_alibi_slopes(h_a)):
        slope = jnp.where(row_s // 2 == h, sl * LOG2E, slope)
    lane_b = lax.broadcasted_iota(jnp.int32, (h_b, page), 1)

    contract_last = (((1,), (1,)), ((), ()))
    qa8, qb8 = qas_ref[...], qbs_ref[...]
    sa_parts, sb_parts = [], []
    carry = carry_ref[...]
    for g in range(g_pages):
        pidx = n_pages - 1 - (s_id * g_pages + g)
        sa = lax.dot_general(qa8, ak[g][...].astype(BF16), contract_last, preferred_element_type=F32)
        dist = (past - pidx * page - lane_s // h_a).astype(F32)
        sa_parts.append(jnp.where(valid_a, sa - slope * dist, NEG_INF))
        sb = jnp.dot(qb8, bk[g][...].astype(BF16), preferred_element_type=F32)
        lt = lfp[g][...]
        x = lt
        k = 1
        while k < page:
            x = x + jnp.where(lane_b + k < page, pltpu.roll(x, page - k, axis=1), 0.0)
            k *= 2
        sb_parts.append(sb + (carry + x - lt) * LOG2E)
        carry = carry + bcast(x[:, 0:1])
    carry_ref[...] = carry

    def update(s_parts, m_ref, l_ref):
        s = jnp.concatenate(s_parts, axis=1)
        m_old = m_ref[...][:, 0:1]
        m_new = jnp.maximum(m_old, jnp.max(s, axis=1, keepdims=True))
        alpha = jnp.exp2(m_old - m_new)
        p = jnp.exp2(s - m_new)
        l_ref[...] = bcast(alpha * l_ref[...][:, 0:1] + jnp.sum(p, axis=1, keepdims=True))
        m_ref[...] = bcast(m_new)
        return alpha, p.astype(BF16)

    alpha_a, pa = update(sa_parts, ma_ref, la_ref)
    va_all = jnp.concatenate([av[g][...].astype(BF16) for g in range(g_pages)], axis=0)
    acca_ref[...] = alpha_a * acca_ref[...] + jnp.dot(pa, va_all, preferred_element_type=F32)
    alpha_b, pb = update(sb_parts, mb_ref, lb_ref)
    vb_all = jnp.concatenate([bv[g][...].astype(BF16) for g in range(g_pages)], axis=1)
    accb_ref[...] = alpha_b * accb_ref[...] + lax.dot_general(pb, vb_all, contract_last,
                                                              preferred_element_type=F32)

    @pl.when(s_id == n_steps - 1)
    def _():
        lam = _diff_lambda(lq1_ref, lk1_ref, lq2_ref, lk2_ref, lam_init)
        inv_a = 1.0 / la_ref[...][:, 0:1]
        w = acca_ref[...] * jnp.where(row_a[:, 0:1] % 2 == 0, inv_a, -lam * inv_a)
        for h in range(h_a):
            o = jnp.sum(jnp.where(row_a // 2 == h, w, 0.0), axis=0, keepdims=True)
            ms = jnp.mean(o * o, axis=1, keepdims=True)
            oa_ref[h:h + 1, :] = o * lax.rsqrt(ms + RMS_EPS) * sub_ref[...]
        inv_b = 1.0 / lb_ref[...][:, 0:1]
        ob_ref[...] = jnp.sum(jnp.where(own_b, accb_ref[...] * inv_b, 0.0), axis=0, keepdims=True)


def _decode(page_table, qa, qb, kna, knb, vna, vnb, lfn, lams, sub, ak2, av2, bkt, bvt, lft, *, layer, lam_init):
    db, h_a, v_a = qa.shape
    w_b = qb.shape[2]
    h_b = lfn.shape[1]
    n_pages = page_table.shape[1]
    page = lft.shape[3]
    g_pages = DECODE_PAGES
    n_steps = n_pages // g_pages

    def samp(a):
        return pl.BlockSpec((None,) + a.shape[1:], lambda b, s, pt: (b,) + (0,) * (a.ndim - 1))

    def paged(a, g):
        return pl.BlockSpec((None, None) + a.shape[2:],
                            lambda b, s, pt: (layer, pt[b, n_pages - 1 - (s * g_pages + g)], 0, 0))

    small = (qa, qb, kna, knb, vna, vnb, lfn)
    consts = (*lams, sub)
    cache_args, cache_specs = [], []
    for g in range(g_pages):
        for a in (ak2, av2, bkt, bvt, lft):
            cache_args.append(a)
            cache_specs.append(paged(a, g))
    n_rows = 2 * h_a
    kern = functools.partial(_decode_kernel, g_pages=g_pages, n_pages=n_pages, page=page, h_a=h_a, h_b=h_b,
                             lam_init=lam_init)
    grid_spec = pltpu.PrefetchScalarGridSpec(
        num_scalar_prefetch=1,
        grid=(db, n_steps),
        in_specs=[samp(a) for a in small] + [_const_spec(a.shape) for a in consts] + cache_specs,
        out_specs=[pl.BlockSpec((None, h_a, v_a), lambda b, s, pt: (b, 0, 0)),
                   pl.BlockSpec((None, 1, w_b), lambda b, s, pt: (b, 0, 0))],
        scratch_shapes=[pltpu.VMEM((n_rows, v_a), BF16), pltpu.VMEM((h_b, w_b), BF16),
                        pltpu.VMEM((n_rows, LANE), F32), pltpu.VMEM((n_rows, LANE), F32),
                        pltpu.VMEM((h_b, LANE), F32), pltpu.VMEM((h_b, LANE), F32),
                        pltpu.VMEM((n_rows, v_a), F32), pltpu.VMEM((h_b, w_b), F32),
                        pltpu.VMEM((h_b, LANE), F32)],
    )
    return pl.pallas_call(
        kern,
        grid_spec=grid_spec,
        out_shape=[jax.ShapeDtypeStruct((db, h_a, v_a), F32), jax.ShapeDtypeStruct((db, 1, w_b), F32)],
        compiler_params=_params(2),
        name="decode",
    )(page_table, *small, *consts, *cache_args)


def _finish_kernel(x_ref, o_ref, wo_ref, g2_ref, wup_ref, wdn_ref, y_ref):
    x1 = x_ref[...] + jnp.dot(o_ref[...], wo_ref[...], preferred_element_type=F32)
    ms = jnp.mean(x1 * x1, axis=-1, keepdims=True)
    hn = (x1 * lax.rsqrt(ms + RMS_EPS) * g2_ref[...]).astype(BF16)
    acc = x1
    for c in range(0, wup_ref.shape[1], FF_CHUNK):
        u = jnp.maximum(jnp.dot(hn, wup_ref[:, c:c + FF_CHUNK], preferred_element_type=F32), 0.0)
        acc = acc + jnp.dot((u * u).astype(BF16), wdn_ref[c:c + FF_CHUNK, :], preferred_element_type=F32)
    y_ref[...] = acc


def _finish(x, o, w_o, g2, w_up, w_down, *, tr):
    r, d = x.shape
    rows = lambda w: pl.BlockSpec((tr, w), lambda i: (i, 0))
    return pl.pallas_call(
        _finish_kernel,
        grid=(r // tr,),
        in_specs=[rows(d), rows(o.shape[1]), _const_spec(w_o.shape), _const_spec(g2.shape),
                  _const_spec(w_up.shape), _const_spec(w_down.shape)],
        out_specs=rows(d),
        out_shape=jax.ShapeDtypeStruct((r, d), F32),
        compiler_params=_params(1),
        name="finish",
    )(x, o, w_o, g2, w_up, w_down)


def kernel(x_prompt, x_sample, cache_a_k, cache_a_v, cache_b_k, cache_b_v, cache_b_logf, page_table, meta_tokens,
           norm1, w_in, b_f, qn_a, kn_a, qn_b, kn_b, lam_q1, lam_k1, lam_q2, lam_k2, subln, w_o, norm2, w_up,
           w_down):
    batch, seq, d = x_prompt.shape
    db, dec_seq, _ = x_sample.shape
    depth, _, page, h_a, v_a = cache_a_k.shape
    h_b = cache_b_k.shape[3]
    w_a, w_b = h_a * v_a, h_b * HEAD_DIM
    wq = w_a + w_b
    n_small = LANE
    assert dec_seq == 1 and v_a == 2 * HEAD_DIM and meta_tokens.shape[0] == N_META
    assert w_a == w_b and w_a % MXU_DIM == 0 and h_a + h_b <= F_LANES
    assert SLOT * h_a <= LANE and SLOT * h_b <= LANE and N_META + db <= n_small
    assert seq % ATTN_TILE == 0 and seq % ROW_TILE == 0 and page_table.shape[1] % DECODE_PAGES == 0
    assert w_in.shape[2] == 3 * wq + h_b

    gmat = _group_sum_matrix()
    pmat, ones_row = _bias_placement(h_a, h_b)
    avec = np.zeros((1, F_LANES), np.float32)
    avec[0, h_b:h_b + h_a] = [s * LOG2E for s in _alibi_slopes(h_a)]
    avec = jnp.asarray(avec)
    q_scale = HEAD_DIM ** -0.5 * LOG2E

    n_pool = cache_a_k.shape[1]
    ak2 = cache_a_k.reshape(depth, n_pool, page * h_a, v_a)
    av2 = cache_a_v.reshape(depth, n_pool, page * h_a, v_a)
    bkt = cache_b_k.transpose(0, 1, 3, 4, 2).reshape(depth, n_pool, w_b, page)
    bvt = cache_b_v.transpose(0, 1, 3, 4, 2).reshape(depth, n_pool, w_b, page)
    lft = cache_b_logf.transpose(0, 1, 3, 2)

    x_main = x_prompt.reshape(batch * seq, d)
    x_small = jnp.concatenate([meta_tokens.astype(F32), x_sample.reshape(db, d),
                               jnp.zeros((n_small - N_META - db, d), F32)], axis=0)
    samples = slice(N_META, N_META + db)
    zero_c0 = jnp.zeros((1, F_LANES), F32)

    outs = {name: [] for name in ("p_ak", "p_av", "p_bk", "p_bv", "p_lf", "s_ak", "s_av", "s_bk", "s_bv", "s_lf")}
    for l in range(depth):
        lam_init = _lambda_init(l)
        w_main = w_in[l, :, :3 * wq].astype(BF16)
        w_f = jnp.pad(w_in[l, :, 3 * wq:], ((0, 0), (0, F_LANES - h_b))).astype(BF16)
        bias_f = jnp.pad(b_f[l], (0, F_LANES - h_b)).reshape(1, F_LANES)
        g1 = norm1[l].reshape(1, d)
        scales = jnp.stack([jnp.tile(qn_a[l], w_a // HEAD_DIM) * q_scale, jnp.tile(kn_a[l], w_a // HEAD_DIM),
                            jnp.tile(qn_b[l], w_b // HEAD_DIM) * q_scale, jnp.tile(kn_b[l], w_b // HEAD_DIM)])
        lams = tuple(a[l].reshape(1, HEAD_DIM) for a in (lam_q1, lam_k1, lam_q2, lam_k2))
        sub_row = (subln[l] * (1.0 - lam_init)).reshape(1, v_a)
        osc = jnp.concatenate([jnp.tile(sub_row, (1, h_a)), jnp.ones((1, w_b), F32)], axis=1)
        proj = functools.partial(_proj, g1=g1, w_main=w_main, w_f=w_f, b_f=bias_f, scales=scales, gmat=gmat,
                                 pmat=pmat, ones_row=ones_row, avec=avec, w_a=w_a, w_b=w_b, h_b=h_b)

        (ka_s, va_s, kb_s, vb_s, lf_s, cum_s, k_s, e_s, qt_s, vt_s, ft_s) = proj(
            x_small, c0=zero_c0, tr=n_small, tiles_per_seq=1, pos0=0)
        (ka_m, va_m, kb_m, vb_m, lf_m, _, k_m, e_m, qt_m, vt_m, ft_m) = proj(
            x_main, c0=cum_s[N_META - 1:N_META], tr=ROW_TILE, tiles_per_seq=seq // ROW_TILE, pos0=N_META)

        o_meta = _attn_meta(qt_s, ft_s, k_s, e_s, vt_s, lams, osc, h_a=h_a, h_b=h_b, lam_init=lam_init)
        q_rows = qt_s[:, samples].T.astype(F32)
        oa_s, ob_s = _decode(
            page_table, q_rows[:, :w_a].reshape(db, h_a, v_a), q_rows[:, w_a:].reshape(db, 1, w_b),
            ka_s[samples].reshape(db, h_a, v_a), kb_s[samples].reshape(db, 1, w_b),
            va_s[samples].reshape(db, h_a, v_a), vb_s[samples].reshape(db, 1, w_b),
            lf_s[samples, :h_b].reshape(db, h_b, 1), lams, sub_row,
            ak2, av2, bkt, bvt, lft, layer=l, lam_init=lam_init)
        o_main = _attn_main(qt_m, ft_m, k_m, e_m, vt_m, k_s, e_s, vt_s, lams, osc, batch=batch, tile=ATTN_TILE,
                            h_a=h_a, h_b=h_b, lam_init=lam_init)
        o_samples = jnp.concatenate([oa_s.reshape(db, w_a), ob_s.reshape(db, w_b)], axis=1).astype(BF16)
        o_small = jnp.concatenate([o_meta[:N_META], o_samples, jnp.zeros((n_small - N_META - db, wq), BF16)],
                                  axis=0)

        fin = functools.partial(_finish, w_o=w_o[l].astype(BF16), g2=norm2[l].reshape(1, d),
                                w_up=w_up[l].astype(BF16), w_down=w_down[l].astype(BF16))
        x_small = fin(x_small, o_small, tr=n_small)
        x_main = fin(x_main, o_main, tr=ROW_TILE)

        def prompt_rows(small, main, tail):
            meta = jnp.broadcast_to(small[:N_META][None], (batch, N_META, small.shape[1]))
            full = jnp.concatenate([meta, main.reshape(batch, seq, main.shape[1])], axis=1)
            return full.reshape((batch, N_META + seq) + tail)

        outs["p_ak"].append(prompt_rows(ka_s, ka_m, (h_a, v_a)))
        outs["p_av"].append(prompt_rows(va_s, va_m, (h_a, v_a)))
        outs["p_bk"].append(prompt_rows(kb_s, kb_m, (h_b, HEAD_DIM)))
        outs["p_bv"].append(prompt_rows(vb_s, vb_m, (h_b, HEAD_DIM)))
        outs["p_lf"].append(prompt_rows(lf_s[:, :h_b], lf_m[:, :h_b], (h_b,)))
        outs["s_ak"].append(ka_s[samples].reshape(db, 1, h_a, v_a))
        outs["s_av"].append(va_s[samples].reshape(db, 1, h_a, v_a))
        outs["s_bk"].append(kb_s[samples].reshape(db, 1, h_b, HEAD_DIM))
        outs["s_bv"].append(vb_s[samples].reshape(db, 1, h_b, HEAD_DIM))
        outs["s_lf"].append(lf_s[samples, :h_b].reshape(db, 1, h_b))

    y_prompt = x_main.reshape(batch, seq, d)
    y_sample = x_small[samples].reshape(db, 1, d)
    stack = lambda name: jnp.stack(outs[name])
    return (y_prompt, y_sample, stack("p_ak"), stack("p_av"), stack("p_bk"), stack("p_bv"), stack("p_lf"),
            stack("s_ak"), stack("s_av"), stack("s_bk"), stack("s_bv"), stack("s_lf"))
```

```python
import functools
import math

import numpy as np
import jax
import jax.numpy as jnp
from jax import lax
from jax.experimental import pallas as pl
from jax.experimental.pallas import tpu as pltpu

F32 = jnp.float32
BF16 = jnp.bfloat16

HEAD_DIM = 64
N_META = 16
RMS_EPS = 1e-6
NEG_INF = -1e30
F_LANES = 128
LOG2E = math.log2(math.e)

LANE = 128
SUBLANES = 8
MXU_DIM = 256
BF16_SUBLANES = 16
VMEM_LIMIT = 56 * 1024 * 1024

SLOT = BF16_SUBLANES
N_PIECES = 3

ROW_TILE = 512
ATTN_TILE = 512
DECODE_PAGES = 8
FF_CHUNK = 1024
SCORE_LOOKAHEAD = 2


def _lambda_init(layer):
    return 0.8 - 0.6 * math.exp(-0.3 * layer)


def _alibi_slopes(h_a):
    return [2.0 ** (-8.0 * (i + 1) / h_a) for i in range(h_a)]


def _const_spec(shape):
    return pl.BlockSpec(shape, lambda *_: (0,) * len(shape))


def _params(n_axes):
    return pltpu.CompilerParams(dimension_semantics=("arbitrary",) * n_axes, vmem_limit_bytes=VMEM_LIMIT)


def _group_sum_matrix():
    idx = np.arange(MXU_DIM) // HEAD_DIM
    return jnp.asarray((idx[:, None] == idx[None, :]).astype(np.float32), BF16)


def _bias_placement(h_a, h_b):
    pm = np.zeros((N_PIECES * F_LANES, 4 * LANE), np.float32)
    ones = np.zeros((1, 4 * LANE), np.float32)
    for h in range(h_a):
        for p in range(N_PIECES):
            pm[p * F_LANES + h_b + h, 0 * LANE + SLOT * h + p] = 1.0
            pm[p * F_LANES + h_b + h, 2 * LANE + SLOT * h + N_PIECES + p] = -1.0
            ones[0, 0 * LANE + SLOT * h + N_PIECES + p] = 1.0
            ones[0, 2 * LANE + SLOT * h + p] = 1.0
    for h in range(h_b):
        for p in range(N_PIECES):
            pm[p * F_LANES + h, 1 * LANE + SLOT * h + p] = -1.0
            pm[p * F_LANES + h, 3 * LANE + SLOT * h + N_PIECES + p] = 1.0
            ones[0, 1 * LANE + SLOT * h + N_PIECES + p] = 1.0
            ones[0, 3 * LANE + SLOT * h + p] = 1.0
    return jnp.asarray(pm, BF16), jnp.asarray(ones, F32)


def _proj_kernel(x_ref, g1_ref, w_ref, wf_ref, bf_ref, sc_ref, gm_ref, pm_ref, ones_ref, avec_ref, c0_ref,
                 ka_ref, va_ref, kb_ref, vb_ref, lf_ref, cum_ref, k_ref, e_ref, qt_ref, vt_ref, ft_ref,
                 carry_ref, *, tiles_per_seq, pos0, w_a, w_b, h_b):
    i = pl.program_id(0)
    tr = x_ref.shape[0]
    seq_tile = i % tiles_per_seq

    x = x_ref[...]
    ms = jnp.mean(x * x, axis=-1, keepdims=True)
    h = (x * lax.rsqrt(ms + RMS_EPS) * g1_ref[...]).astype(BF16)

    def seg(off, width):
        return jnp.dot(h, w_ref[:, off:off + width], preferred_element_type=F32)

    def headnorm(p, scale_row):
        outs = []
        for c in range(0, p.shape[1], MXU_DIM):
            pc = p[:, c:c + MXU_DIM]
            ssq = jnp.dot((pc * pc).astype(BF16), gm_ref[...], preferred_element_type=F32)
            outs.append(pc * lax.rsqrt(ssq * (1.0 / HEAD_DIM) + RMS_EPS) * scale_row[:, c:c + MXU_DIM])
        return jnp.concatenate(outs, axis=1)

    qa = headnorm(seg(0, w_a), sc_ref[0:1, :])
    qt_ref[0:w_a, :] = qa.T.astype(BF16)
    ka = headnorm(seg(w_a, w_a), sc_ref[1:2, :])
    ka_ref[...] = ka
    k_ref[:, 0:w_a] = ka.astype(BF16)
    va = seg(2 * w_a, w_a)
    va_ref[...] = va
    vt_ref[0:w_a, :] = va.T.astype(BF16)
    off = 3 * w_a
    qb = headnorm(seg(off, w_b), sc_ref[2:3, :])
    qt_ref[w_a:w_a + w_b, :] = qb.T.astype(BF16)
    kb = headnorm(seg(off + w_b, w_b), sc_ref[3:4, :])
    kb_ref[...] = kb
    k_ref[:, w_a:w_a + w_b] = kb.astype(BF16)
    vb = seg(off + 2 * w_b, w_b)
    vb_ref[...] = vb
    vt_ref[w_a:w_a + w_b, :] = vb.T.astype(BF16)

    logit = jnp.dot(h, wf_ref[...], preferred_element_type=F32) + bf_ref[...]
    lf = jax.nn.log_sigmoid(logit)
    lf_ref[...] = lf

    @pl.when(seq_tile == 0)
    def _():
        carry_ref[...] = c0_ref[...]

    row = lax.broadcasted_iota(jnp.int32, lf.shape, 0)
    lane = lax.broadcasted_iota(jnp.int32, lf.shape, 1)
    c = lf
    k = 1
    while k < tr:
        c = c + jnp.where(row >= k, pltpu.roll(c, k, axis=0), 0.0)
        k *= 2
    c = c + carry_ref[...]
    carry_ref[...] = c[tr - 1:tr, :]
    cum_ref[...] = c

    pos = (row + (pos0 + seq_tile * tr)).astype(F32)
    z = jnp.where(lane < h_b, c * LOG2E, avec_ref[...] * pos)
    pieces = []
    for _ in range(N_PIECES):
        zp = z.astype(BF16)
        pieces.append(zp)
        z = z - zp.astype(F32)
    ef = jnp.dot(jnp.concatenate(pieces, axis=1), pm_ref[...], preferred_element_type=F32) + ones_ref[...]
    e_ref[...] = ef[:, :2 * LANE].astype(BF16)
    ft_ref[...] = ef[:, 2 * LANE:].T.astype(BF16)


def _proj(x, g1, w_main, w_f, b_f, scales, gmat, pmat, ones_row, avec, c0, *, tr, tiles_per_seq, pos0, w_a, w_b,
          h_b):
    r, d = x.shape
    n = r // tr
    wq = w_a + w_b
    sds = jax.ShapeDtypeStruct
    rows = lambda w: pl.BlockSpec((tr, w), lambda i: (i, 0))
    cols = lambda w: pl.BlockSpec((w, tr), lambda i: (0, i))
    kern = functools.partial(_proj_kernel, tiles_per_seq=tiles_per_seq, pos0=pos0, w_a=w_a, w_b=w_b, h_b=h_b)
    return pl.pallas_call(
        kern,
        grid=(n,),
        in_specs=[rows(d), _const_spec(g1.shape), _const_spec(w_main.shape), _const_spec(w_f.shape),
                  _const_spec(b_f.shape), _const_spec(scales.shape), _const_spec(gmat.shape),
                  _const_spec(pmat.shape), _const_spec(ones_row.shape), _const_spec(avec.shape),
                  _const_spec(c0.shape)],
        out_specs=[rows(w_a), rows(w_a), rows(w_b), rows(w_b), rows(F_LANES), rows(F_LANES),
                   rows(wq), rows(2 * LANE), cols(wq), cols(wq), cols(2 * LANE)],
        out_shape=[sds((r, w_a), F32), sds((r, w_a), F32), sds((r, w_b), F32), sds((r, w_b), F32),
                   sds((r, F_LANES), F32), sds((r, F_LANES), F32),
                   sds((r, wq), BF16), sds((r, 2 * LANE), BF16), sds((wq, r), BF16), sds((wq, r), BF16),
                   sds((2 * LANE, r), BF16)],
        scratch_shapes=[pltpu.VMEM((1, F_LANES), F32)],
        compiler_params=_params(1),
        name="proj",
    )(x, g1, w_main, w_f, b_f, scales, gmat, pmat, ones_row, avec, c0)


def _head_maps(h_a, h_b):
    w_a = h_a * 2 * HEAD_DIM
    maps = []
    for r in range(2 * h_a):
        h, m = divmod(r, 2)
        maps.append(dict(q_row=r * HEAD_DIM, half=m, k_col=h * LANE, e_col=0, f_row=SLOT * h,
                         v_row=h * LANE, v_dim=2 * HEAD_DIM, group=0, acc_row=r * LANE))
    for h in range(h_b):
        maps.append(dict(q_row=w_a + h * HEAD_DIM, half=h % 2, k_col=w_a + (h // 2) * LANE, e_col=LANE,
                         f_row=LANE + SLOT * h, v_row=w_a + h * HEAD_DIM, v_dim=HEAD_DIM, group=1,
                         acc_row=h * HEAD_DIM))
    return maps


def _attn_init(qt_ref, ft_ref, qs_ref, m_ref, l_ref, acca_ref, accb_ref, maps):
    m_ref[...] = jnp.full(m_ref.shape, NEG_INF, F32)
    l_ref[...] = jnp.zeros(l_ref.shape, F32)
    acca_ref[...] = jnp.zeros(acca_ref.shape, F32)
    accb_ref[...] = jnp.zeros(accb_ref.shape, F32)
    qs_ref[...] = jnp.zeros(qs_ref.shape, qs_ref.dtype)
    for r, mp in enumerate(maps):
        base = r * 2 * LANE
        q_dst = base + mp["half"] * HEAD_DIM
        qs_ref[q_dst:q_dst + HEAD_DIM, :] = qt_ref[mp["q_row"]:mp["q_row"] + HEAD_DIM, :]
        f_dst = base + LANE + mp["f_row"] % LANE
        qs_ref[f_dst:f_dst + SLOT, :] = ft_ref[mp["f_row"]:mp["f_row"] + SLOT, :]


def _attn_tile(k_ref, e_ref, vt_ref, mask, qs_ref, m_ref, l_ref, acca_ref, accb_ref, maps):
    def scores(r):
        mp = maps[r]
        kaug = jnp.concatenate([k_ref[:, mp["k_col"]:mp["k_col"] + LANE],
                                e_ref[:, mp["e_col"]:mp["e_col"] + LANE]], axis=1)
        return jnp.dot(kaug, qs_ref[r * 2 * LANE:(r + 1) * 2 * LANE, :], preferred_element_type=F32)

    pending = [scores(r) for r in range(min(SCORE_LOOKAHEAD, len(maps)))]
    for r, mp in enumerate(maps):
        s = pending.pop(0)
        if r + SCORE_LOOKAHEAD < len(maps):
            pending.append(scores(r + SCORE_LOOKAHEAD))
        if mask is not None:
            s = jnp.where(mask, s, NEG_INF)
        tk, tq = s.shape
        m_old = m_ref[r]
        m_new = jnp.maximum(m_old, jnp.max(s, axis=0, keepdims=True))
        alpha = jnp.exp2(m_old - m_new)
        p3 = jnp.exp2(s.reshape(tk // SUBLANES, SUBLANES, tq) - m_new[None])
        l_ref[r] = alpha * l_ref[r] + jnp.sum(p3, axis=0)
        m_ref[r] = m_new
        acc_ref = accb_ref if mp["group"] else acca_ref
        a0, d = mp["acc_row"], mp["v_dim"]
        pv = jnp.dot(vt_ref[mp["v_row"]:mp["v_row"] + d, :], p3.reshape(tk, tq).astype(BF16),
                     preferred_element_type=F32)
        acc3 = acc_ref[a0:a0 + d, :].reshape(d // SUBLANES, SUBLANES, tq)
        acc_ref[a0:a0 + d, :] = (alpha[None] * acc3).reshape(d, tq) + pv


def _diff_lambda(lq1_ref, lk1_ref, lq2_ref, lk2_ref, lam_init):
    return (jnp.exp(jnp.sum(lq1_ref[...] * lk1_ref[...], axis=1, keepdims=True))
            - jnp.exp(jnp.sum(lq2_ref[...] * lk2_ref[...], axis=1, keepdims=True)) + lam_init)


def _attn_finalize(o_ref, osc_ref, lam, l_ref, acca_ref, accb_ref, h_a, h_b):
    inv_l = 1.0 / jnp.sum(l_ref[...], axis=1)
    outs = []
    for h in range(h_a):
        r1, r2 = 2 * h, 2 * h + 1
        o = (acca_ref[r1 * LANE:(r1 + 1) * LANE, :] * inv_l[r1:r1 + 1, :]
             - lam * (acca_ref[r2 * LANE:(r2 + 1) * LANE, :] * inv_l[r2:r2 + 1, :]))
        ms = jnp.mean(o * o, axis=0, keepdims=True)
        outs.append(o * lax.rsqrt(ms + RMS_EPS))
    for h in range(h_b):
        r = 2 * h_a + h
        outs.append(accb_ref[h * HEAD_DIM:(h + 1) * HEAD_DIM, :] * inv_l[r:r + 1, :])
    ot = jnp.concatenate(outs, axis=0)
    o_ref[...] = (ot.T * osc_ref[...]).astype(o_ref.dtype)


def _attn_main_kernel(qtab_ref, ktab_ref, qt_ref, ft_ref, k_ref, e_ref, vt_ref, km_ref, em_ref, vtm_ref,
                      lq1_ref, lk1_ref, lq2_ref, lk2_ref, osc_ref, o_ref,
                      qs_ref, m_ref, l_ref, acca_ref, accb_ref, *, h_a, h_b, lam_init):
    t = pl.program_id(1)
    qi = qtab_ref[t]
    ki = ktab_ref[t]
    maps = _head_maps(h_a, h_b)
    state = (qs_ref, m_ref, l_ref, acca_ref, accb_ref)
    tk, tq = k_ref.shape[0], qt_ref.shape[1]

    @pl.when(ki == 0)
    def _():
        _attn_init(qt_ref, ft_ref, *state, maps)
        key = lax.broadcasted_iota(jnp.int32, (km_ref.shape[0], tq), 0)
        _attn_tile(km_ref, em_ref, vtm_ref, key < N_META, *state, maps)

    @pl.when(ki < qi)
    def _():
        _attn_tile(k_ref, e_ref, vt_ref, None, *state, maps)

    @pl.when(ki == qi)
    def _():
        key = lax.broadcasted_iota(jnp.int32, (tk, tq), 0)
        qry = lax.broadcasted_iota(jnp.int32, (tk, tq), 1)
        _attn_tile(k_ref, e_ref, vt_ref, key <= qry, *state, maps)
        lam = _diff_lambda(lq1_ref, lk1_ref, lq2_ref, lk2_ref, lam_init)
        _attn_finalize(o_ref, osc_ref, lam, l_ref, acca_ref, accb_ref, h_a, h_b)


def _attn_meta_kernel(qt_ref, ft_ref, km_ref, em_ref, vtm_ref, lq1_ref, lk1_ref, lq2_ref, lk2_ref, osc_ref, o_ref,
                      qs_ref, m_ref, l_ref, acca_ref, accb_ref, *, h_a, h_b, lam_init):
    maps = _head_maps(h_a, h_b)
    state = (qs_ref, m_ref, l_ref, acca_ref, accb_ref)
    tk, tq = km_ref.shape[0], qt_ref.shape[1]
    _attn_init(qt_ref, ft_ref, *state, maps)
    key = lax.broadcasted_iota(jnp.int32, (tk, tq), 0)
    qry = lax.broadcasted_iota(jnp.int32, (tk, tq), 1)
    _attn_tile(km_ref, em_ref, vtm_ref, (key <= qry) & (key < N_META), *state, maps)
    lam = _diff_lambda(lq1_ref, lk1_ref, lq2_ref, lk2_ref, lam_init)
    _attn_finalize(o_ref, osc_ref, lam, l_ref, acca_ref, accb_ref, h_a, h_b)


def _attn_scratch(tq, h_a, h_b):
    n_maps = 2 * h_a + h_b
    return [pltpu.VMEM((n_maps * 2 * LANE, tq), BF16),
            pltpu.VMEM((n_maps, SUBLANES, tq), F32), pltpu.VMEM((n_maps, SUBLANES, tq), F32),
            pltpu.VMEM((2 * h_a * LANE, tq), F32), pltpu.VMEM((h_b * HEAD_DIM, tq), F32)]


def _attn_main(qt, ft, k, e, vt, km, em, vtm, lams, osc, *, batch, tile, h_a, h_b, lam_init):
    wq, r = qt.shape
    n = r // batch // tile
    pairs = [(qi, ki) for qi in range(n) for ki in range(qi + 1)]
    q_tab = jnp.asarray([p[0] for p in pairs], jnp.int32)
    k_tab = jnp.asarray([p[1] for p in pairs], jnp.int32)
    q_blk = lambda b, t, qtab, ktab: b * n + qtab[t]
    k_blk = lambda b, t, qtab, ktab: b * n + ktab[t]
    in_specs = [
        pl.BlockSpec((wq, tile), lambda *a: (0, q_blk(*a))),
        pl.BlockSpec((2 * LANE, tile), lambda *a: (0, q_blk(*a))),
        pl.BlockSpec((tile, wq), lambda *a: (k_blk(*a), 0)),
        pl.BlockSpec((tile, 2 * LANE), lambda *a: (k_blk(*a), 0)),
        pl.BlockSpec((wq, tile), lambda *a: (0, k_blk(*a))),
        _const_spec(km.shape), _const_spec(em.shape), _const_spec(vtm.shape),
    ] + [_const_spec(a.shape) for a in lams] + [_const_spec(osc.shape)]
    kern = functools.partial(_attn_main_kernel, h_a=h_a, h_b=h_b, lam_init=lam_init)
    grid_spec = pltpu.PrefetchScalarGridSpec(
        num_scalar_prefetch=2,
        grid=(batch, len(pairs)),
        in_specs=in_specs,
        out_specs=pl.BlockSpec((tile, wq), lambda *a: (q_blk(*a), 0)),
        scratch_shapes=_attn_scratch(tile, h_a, h_b),
    )
    return pl.pallas_call(
        kern,
        grid_spec=grid_spec,
        out_shape=jax.ShapeDtypeStruct((r, wq), BF16),
        compiler_params=_params(2),
        name="attn_main",
    )(q_tab, k_tab, qt, ft, k, e, vt, km, em, vtm, *lams, osc)


def _attn_meta(qt, ft, km, em, vtm, lams, osc, *, h_a, h_b, lam_init):
    wq, r = qt.shape
    args = (qt, ft, km, em, vtm, *lams, osc)
    kern = functools.partial(_attn_meta_kernel, h_a=h_a, h_b=h_b, lam_init=lam_init)
    return pl.pallas_call(
        kern,
        grid=(1,),
        in_specs=[_const_spec(a.shape) for a in args],
        out_specs=_const_spec((r, wq)),
        out_shape=jax.ShapeDtypeStruct((r, wq), BF16),
        scratch_shapes=_attn_scratch(r, h_a, h_b),
        compiler_params=_params(1),
        name="attn_meta",
    )(*args)


def _decode_kernel(pt_ref, qa_ref, qb_ref, kna_ref, knb_ref, vna_ref, vnb_ref, lfn_ref,
                   lq1_ref, lk1_ref, lq2_ref, lk2_ref, sub_ref, *rest,
                   g_pages, n_pages, page, h_a, h_b, lam_init):
    caches = rest[:5 * g_pages]
    oa_ref, ob_ref = rest[5 * g_pages:5 * g_pages + 2]
    qas_ref, qbs_ref, ma_ref, la_ref, mb_ref, lb_ref, acca_ref, accb_ref, carry_ref = rest[5 * g_pages + 2:]
    ak, av, bk, bv, lfp = ([caches[5 * g + j] for g in range(g_pages)] for j in range(5))
    s_id = pl.program_id(1)
    n_steps = n_pages // g_pages
    n_rows = 2 * h_a
    past = n_pages * page
    v_a = 2 * HEAD_DIM
    w_b = h_b * HEAD_DIM
    a_keys = page * h_a

    bcast = lambda col: jnp.broadcast_to(col, (col.shape[0], LANE))
    row_a = lax.broadcasted_iota(jnp.int32, (n_rows, v_a), 0)
    lane_a = lax.broadcasted_iota(jnp.int32, (n_rows, v_a), 1)
    row_b = lax.broadcasted_iota(jnp.int32, (h_b, w_b), 0)
    own_b = lax.broadcasted_iota(jnp.int32, (h_b, w_b), 1) // HEAD_DIM == row_b

    def per_head_rows(ref):
        out = jnp.zeros((n_rows, v_a), F32)
        for h in range(h_a):
            out = jnp.where(row_a // 2 == h, ref[h:h + 1, :], out)
        return out

    @pl.when(s_id == 0)
    def _():
        half_sel = (lane_a // HEAD_DIM) == (row_a % 2)
        q8 = jnp.where(half_sel, per_head_rows(qa_ref), 0.0)
        qas_ref[...] = q8.astype(BF16)
        qblk = jnp.where(own_b, qb_ref[...], 0.0)
        qbs_ref[...] = qblk.astype(BF16)
        ma_ref[...] = bcast(jnp.sum(q8 * per_head_rows(kna_ref), axis=1, keepdims=True))
        mb_ref[...] = bcast(jnp.sum(qblk * knb_ref[...], axis=1, keepdims=True))
        la_ref[...] = jnp.ones(la_ref.shape, F32)
        lb_ref[...] = jnp.ones(lb_ref.shape, F32)
        acca_ref[...] = per_head_rows(vna_ref)
        accb_ref[...] = jnp.broadcast_to(vnb_ref[...], (h_b, w_b))
        carry_ref[...] = bcast(lfn_ref[...])

    row_s = lax.broadcasted_iota(jnp.int32, (n_rows, a_keys), 0)
    lane_s = lax.broadcasted_iota(jnp.int32, (n_rows, a_keys), 1)
    valid_a = (lane_s % h_a) == (row_s // 2)
    slope = jnp.zeros((n_rows, a_keys), F32)
    for h, sl in enumerate(_alibi_slopes(h_a)):
        slope = jnp.where(row_s // 2 == h, sl * LOG2E, slope)
    lane_b = lax.broadcasted_iota(jnp.int32, (h_b, page), 1)

    contract_last = (((1,), (1,)), ((), ()))
    qa8, qb8 = qas_ref[...], qbs_ref[...]
    sa_parts, sb_parts = [], []
    carry = carry_ref[...]
    for g in range(g_pages):
        pidx = n_pages - 1 - (s_id * g_pages + g)
        sa = lax.dot_general(qa8, ak[g][...].astype(BF16), contract_last, preferred_element_type=F32)
        dist = (past - pidx * page - lane_s // h_a).astype(F32)
        sa_parts.append(jnp.where(valid_a, sa - slope * dist, NEG_INF))
        sb = jnp.dot(qb8, bk[g][...].astype(BF16), preferred_element_type=F32)
        lt = lfp[g][...]
        x = lt
        k = 1
        while k < page:
            x = x + jnp.where(lane_b + k < page, pltpu.roll(x, page - k, axis=1), 0.0)
            k *= 2
        sb_parts.append(sb + (carry + x - lt) * LOG2E)
        carry = carry + bcast(x[:, 0:1])
    carry_ref[...] = carry

    def update(s_parts, m_ref, l_ref):
        s = jnp.concatenate(s_parts, axis=1)
        m_old = m_ref[...][:, 0:1]
        m_new = jnp.maximum(m_old, jnp.max(s, axis=1, keepdims=True))
        alpha = jnp.exp2(m_old - m_new)
        p = jnp.exp2(s - m_new)
        l_ref[...] = bcast(alpha * l_ref[...][:, 0:1] + jnp.sum(p, axis=1, keepdims=True))
        m_ref[...] = bcast(m_new)
        return alpha, p.astype(BF16)

    alpha_a, pa = update(sa_parts, ma_ref, la_ref)
    va_all = jnp.concatenate([av[g][...].astype(BF16) for g in range(g_pages)], axis=0)
    acca_ref[...] = alpha_a * acca_ref[...] + jnp.dot(pa, va_all, preferred_element_type=F32)
    alpha_b, pb = update(sb_parts, mb_ref, lb_ref)
    vb_all = jnp.concatenate([bv[g][...].astype(BF16) for g in range(g_pages)], axis=1)
    accb_ref[...] = alpha_b * accb_ref[...] + lax.dot_general(pb, vb_all, contract_last,
                                                              preferred_element_type=F32)

    @pl.when(s_id == n_steps - 1)
    def _():
        lam = _diff_lambda(lq1_ref, lk1_ref, lq2_ref, lk2_ref, lam_init)
        inv_a = 1.0 / la_ref[...][:, 0:1]
        w = acca_ref[...] * jnp.where(row_a[:, 0:1] % 2 == 0, inv_a, -lam * inv_a)
        for h in range(h_a):
            o = jnp.sum(jnp.where(row_a // 2 == h, w, 0.0), axis=0, keepdims=True)
            ms = jnp.mean(o * o, axis=1, keepdims=True)
            oa_ref[h:h + 1, :] = o * lax.rsqrt(ms + RMS_EPS) * sub_ref[...]
        inv_b = 1.0 / lb_ref[...][:, 0:1]
        ob_ref[...] = jnp.sum(jnp.where(own_b, accb_ref[...] * inv_b, 0.0), axis=0, keepdims=True)


def _decode(page_table, qa, qb, kna, knb, vna, vnb, lfn, lams, sub, ak2, av2, bkt, bvt, lft, *, layer, lam_init):
    db, h_a, v_a = qa.shape
    w_b = qb.shape[2]
    h_b = lfn.shape[1]
    n_pages = page_table.shape[1]
    page = lft.shape[3]
    g_pages = DECODE_PAGES
    n_steps = n_pages // g_pages

    def samp(a):
        return pl.BlockSpec((None,) + a.shape[1:], lambda b, s, pt: (b,) + (0,) * (a.ndim - 1))

    def paged(a, g):
        return pl.BlockSpec((None, None) + a.shape[2:],
                            lambda b, s, pt: (layer, pt[b, n_pages - 1 - (s * g_pages + g)], 0, 0))

    small = (qa, qb, kna, knb, vna, vnb, lfn)
    consts = (*lams, sub)
    cache_args, cache_specs = [], []
    for g in range(g_pages):
        for a in (ak2, av2, bkt, bvt, lft):
            cache_args.append(a)
            cache_specs.append(paged(a, g))
    n_rows = 2 * h_a
    kern = functools.partial(_decode_kernel, g_pages=g_pages, n_pages=n_pages, page=page, h_a=h_a, h_b=h_b,
                             lam_init=lam_init)
    grid_spec = pltpu.PrefetchScalarGridSpec(
        num_scalar_prefetch=1,
        grid=(db, n_steps),
        in_specs=[samp(a) for a in small] + [_const_spec(a.shape) for a in consts] + cache_specs,
        out_specs=[pl.BlockSpec((None, h_a, v_a), lambda b, s, pt: (b, 0, 0)),
                   pl.BlockSpec((None, 1, w_b), lambda b, s, pt: (b, 0, 0))],
        scratch_shapes=[pltpu.VMEM((n_rows, v_a), BF16), pltpu.VMEM((h_b, w_b), BF16),
                        pltpu.VMEM((n_rows, LANE), F32), pltpu.VMEM((n_rows, LANE), F32),
                        pltpu.VMEM((h_b, LANE), F32), pltpu.VMEM((h_b, LANE), F32),
                        pltpu.VMEM((n_rows, v_a), F32), pltpu.VMEM((h_b, w_b), F32),
                        pltpu.VMEM((h_b, LANE), F32)],
    )
    return pl.pallas_call(
        kern,
        grid_spec=grid_spec,
        out_shape=[jax.ShapeDtypeStruct((db, h_a, v_a), F32), jax.ShapeDtypeStruct((db, 1, w_b), F32)],
        compiler_params=_params(2),
        name="decode",
    )(page_table, *small, *consts, *cache_args)


def _finish_kernel(x_ref, o_ref, wo_ref, g2_ref, wup_ref, wdn_ref, y_ref):
    x1 = x_ref[...] + jnp.dot(o_ref[...], wo_ref[...], preferred_element_type=F32)
    ms = jnp.mean(x1 * x1, axis=-1, keepdims=True)
    hn = (x1 * lax.rsqrt(ms + RMS_EPS) * g2_ref[...]).astype(BF16)
    acc = x1
    for c in range(0, wup_ref.shape[1], FF_CHUNK):
        u = jnp.maximum(jnp.dot(hn, wup_ref[:, c:c + FF_CHUNK], preferred_element_type=F32), 0.0)
        acc = acc + jnp.dot((u * u).astype(BF16), wdn_ref[c:c + FF_CHUNK, :], preferred_element_type=F32)
    y_ref[...] = acc


def _finish(x, o, w_o, g2, w_up, w_down, *, tr):
    r, d = x.shape
    rows = lambda w: pl.BlockSpec((tr, w), lambda i: (i, 0))
    return pl.pallas_call(
        _finish_kernel,
        grid=(r // tr,),
        in_specs=[rows(d), rows(o.shape[1]), _const_spec(w_o.shape), _const_spec(g2.shape),
                  _const_spec(w_up.shape), _const_spec(w_down.shape)],
        out_specs=rows(d),
        out_shape=jax.ShapeDtypeStruct((r, d), F32),
        compiler_params=_params(1),
        name="finish",
    )(x, o, w_o, g2, w_up, w_down)


def kernel(x_prompt, x_sample, cache_a_k, cache_a_v, cache_b_k, cache_b_v, cache_b_logf, page_table, meta_tokens,
           norm1, w_in, b_f, qn_a, kn_a, qn_b, kn_b, lam_q1, lam_k1, lam_q2, lam_k2, subln, w_o, norm2, w_up,
           w_down):
    batch, seq, d = x_prompt.shape
    db, dec_seq, _ = x_sample.shape
    depth, _, page, h_a, v_a = cache_a_k.shape
    h_b = cache_b_k.shape[3]
    w_a, w_b = h_a * v_a, h_b * HEAD_DIM
    wq = w_a + w_b
    n_small = LANE
    assert dec_seq == 1 and v_a == 2 * HEAD_DIM and meta_tokens.shape[0] == N_META
    assert w_a == w_b and w_a % MXU_DIM == 0 and h_a + h_b <= F_LANES
    assert SLOT * h_a <= LANE and SLOT * h_b <= LANE and N_META + db <= n_small
    assert seq % ATTN_TILE == 0 and seq % ROW_TILE == 0 and page_table.shape[1] % DECODE_PAGES == 0
    assert w_in.shape[2] == 3 * wq + h_b

    gmat = _group_sum_matrix()
    pmat, ones_row = _bias_placement(h_a, h_b)
    avec = np.zeros((1, F_LANES), np.float32)
    avec[0, h_b:h_b + h_a] = [s * LOG2E for s in _alibi_slopes(h_a)]
    avec = jnp.asarray(avec)
    q_scale = HEAD_DIM ** -0.5 * LOG2E

    n_pool = cache_a_k.shape[1]
    ak2 = cache_a_k.reshape(depth, n_pool, page * h_a, v_a)
    av2 = cache_a_v.reshape(depth, n_pool, page * h_a, v_a)
    bkt = cache_b_k.transpose(0, 1, 3, 4, 2).reshape(depth, n_pool, w_b, page)
    bvt = cache_b_v.transpose(0, 1, 3, 4, 2).reshape(depth, n_pool, w_b, page)
    lft = cache_b_logf.transpose(0, 1, 3, 2)

    x_main = x_prompt.reshape(batch * seq, d)
    x_small = jnp.concatenate([meta_tokens.astype(F32), x_sample.reshape(db, d),
                               jnp.zeros((n_small - N_META - db, d), F32)], axis=0)
    samples = slice(N_META, N_META + db)
    zero_c0 = jnp.zeros((1, F_LANES), F32)

    outs = {name: [] for name in ("p_ak", "p_av", "p_bk", "p_bv", "p_lf", "s_ak", "s_av", "s_bk", "s_bv", "s_lf")}
    for l in range(depth):
        lam_init = _lambda_init(l)
        w_main = w_in[l, :, :3 * wq].astype(BF16)
        w_f = jnp.pad(w_in[l, :, 3 * wq:], ((0, 0), (0, F_LANES - h_b))).astype(BF16)
        bias_f = jnp.pad(b_f[l], (0, F_LANES - h_b)).reshape(1, F_LANES)
        g1 = norm1[l].reshape(1, d)
        scales = jnp.stack([jnp.tile(qn_a[l], w_a // HEAD_DIM) * q_scale, jnp.tile(kn_a[l], w_a // HEAD_DIM),
                            jnp.tile(qn_b[l], w_b // HEAD_DIM) * q_scale, jnp.tile(kn_b[l], w_b // HEAD_DIM)])
        lams = tuple(a[l].reshape(1, HEAD_DIM) for a in (lam_q1, lam_k1, lam_q2, lam_k2))
        sub_row = (subln[l] * (1.0 - lam_init)).reshape(1, v_a)
        osc = jnp.concatenate([jnp.tile(sub_row, (1, h_a)), jnp.ones((1, w_b), F32)], axis=1)
        proj = functools.partial(_proj, g1=g1, w_main=w_main, w_f=w_f, b_f=bias_f, scales=scales, gmat=gmat,
                                 pmat=pmat, ones_row=ones_row, avec=avec, w_a=w_a, w_b=w_b, h_b=h_b)

        (ka_s, va_s, kb_s, vb_s, lf_s, cum_s, k_s, e_s, qt_s, vt_s, ft_s) = proj(
            x_small, c0=zero_c0, tr=n_small, tiles_per_seq=1, pos0=0)
        (ka_m, va_m, kb_m, vb_m, lf_m, _, k_m, e_m, qt_m, vt_m, ft_m) = proj(
            x_main, c0=cum_s[N_META - 1:N_META], tr=ROW_TILE, tiles_per_seq=seq // ROW_TILE, pos0=N_META)

        o_meta = _attn_meta(qt_s, ft_s, k_s, e_s, vt_s, lams, osc, h_a=h_a, h_b=h_b, lam_init=lam_init)
        q_rows = qt_s[:, samples].T.astype(F32)
        oa_s, ob_s = _decode(
            page_table, q_rows[:, :w_a].reshape(db, h_a, v_a), q_rows[:, w_a:].reshape(db, 1, w_b),
            ka_s[samples].reshape(db, h_a, v_a), kb_s[samples].reshape(db, 1, w_b),
            va_s[samples].reshape(db, h_a, v_a), vb_s[samples].reshape(db, 1, w_b),
            lf_s[samples, :h_b].reshape(db, h_b, 1), lams, sub_row,
            ak2, av2, bkt, bvt, lft, layer=l, lam_init=lam_init)
        o_main = _attn_main(qt_m, ft_m, k_m, e_m, vt_m, k_s, e_s, vt_s, lams, osc, batch=batch, tile=ATTN_TILE,
                            h_a=h_a, h_b=h_b, lam_init=lam_init)
        o_samples = jnp.concatenate([oa_s.reshape(db, w_a), ob_s.reshape(db, w_b)], axis=1).astype(BF16)
        o_small = jnp.concatenate([o_meta[:N_META], o_samples, jnp.zeros((n_small - N_META - db, wq), BF16)],
                                  axis=0)

        fin = functools.partial(_finish, w_o=w_o[l].astype(BF16), g2=norm2[l].reshape(1, d),
                                w_up=w_up[l].astype(BF16), w_down=w_down[l].astype(BF16))
        x_small = fin(x_small, o_small, tr=n_small)
        x_main = fin(x_main, o_main, tr=ROW_TILE)

        def prompt_rows(small, main):
            pieces = []
            for b in range(batch):
                pieces += [small[:N_META], main[b * seq:(b + 1) * seq]]
            return pieces

        outs["p_ak"] += prompt_rows(ka_s, ka_m)
        outs["p_av"] += prompt_rows(va_s, va_m)
        outs["p_bk"] += prompt_rows(kb_s, kb_m)
        outs["p_bv"] += prompt_rows(vb_s, vb_m)
        outs["p_lf"] += prompt_rows(lf_s[:, :h_b], lf_m[:, :h_b])
        outs["s_ak"].append(ka_s[samples].reshape(db, 1, h_a, v_a))
        outs["s_av"].append(va_s[samples].reshape(db, 1, h_a, v_a))
        outs["s_bk"].append(kb_s[samples].reshape(db, 1, h_b, HEAD_DIM))
        outs["s_bv"].append(vb_s[samples].reshape(db, 1, h_b, HEAD_DIM))
        outs["s_lf"].append(lf_s[samples, :h_b].reshape(db, 1, h_b))

    y_prompt = x_main.reshape(batch, seq, d)
    y_sample = x_small[samples].reshape(db, 1, d)
    stack = lambda name: jnp.stack(outs[name])
    rows = lambda name, tail: jnp.concatenate(outs[name], axis=0).reshape((depth, batch, N_META + seq) + tail)
    return (y_prompt, y_sample, rows("p_ak", (h_a, v_a)), rows("p_av", (h_a, v_a)), rows("p_bk", (h_b, HEAD_DIM)),
            rows("p_bv", (h_b, HEAD_DIM)), rows("p_lf", (h_b,)),
            stack("s_ak"), stack("s_av"), stack("s_bk"), stack("s_bv"), stack("s_lf"))
```

```python
import functools
import math

import numpy as np
import jax
import jax.numpy as jnp
from jax import lax
from jax.experimental import pallas as pl
from jax.experimental.pallas import tpu as pltpu

F32 = jnp.float32
BF16 = jnp.bfloat16

HEAD_DIM = 64
N_META = 16
RMS_EPS = 1e-6
NEG_INF = -1e30
F_LANES = 128
LOG2E = math.log2(math.e)

LANE = 128
SUBLANES = 8
MXU_DIM = 256
BF16_SUBLANES = 16
VMEM_LIMIT = 56 * 1024 * 1024

SLOT = BF16_SUBLANES
N_PIECES = 3

ROW_TILE = 512
ATTN_TILE = 512
DEC_CHUNK = 4
DEC_SLOTS = 4
FF_CHUNK = 1024
SCORE_LOOKAHEAD = 2


def _lambda_init(layer):
    return 0.8 - 0.6 * math.exp(-0.3 * layer)


def _alibi_slopes(h_a):
    return [2.0 ** (-8.0 * (i + 1) / h_a) for i in range(h_a)]


def _const_spec(shape):
    return pl.BlockSpec(shape, lambda *_: (0,) * len(shape))


def _params(n_axes):
    return pltpu.CompilerParams(dimension_semantics=("arbitrary",) * n_axes, vmem_limit_bytes=VMEM_LIMIT)


def _group_sum_matrix():
    idx = np.arange(MXU_DIM) // HEAD_DIM
    return jnp.asarray((idx[:, None] == idx[None, :]).astype(np.float32), BF16)


def _bias_placement(h_a, h_b):
    pm = np.zeros((N_PIECES * F_LANES, 4 * LANE), np.float32)
    ones = np.zeros((1, 4 * LANE), np.float32)
    for h in range(h_a):
        for p in range(N_PIECES):
            pm[p * F_LANES + h_b + h, 0 * LANE + SLOT * h + p] = 1.0
            pm[p * F_LANES + h_b + h, 2 * LANE + SLOT * h + N_PIECES + p] = -1.0
            ones[0, 0 * LANE + SLOT * h + N_PIECES + p] = 1.0
            ones[0, 2 * LANE + SLOT * h + p] = 1.0
    for h in range(h_b):
        for p in range(N_PIECES):
            pm[p * F_LANES + h, 1 * LANE + SLOT * h + p] = -1.0
            pm[p * F_LANES + h, 3 * LANE + SLOT * h + N_PIECES + p] = 1.0
            ones[0, 1 * LANE + SLOT * h + N_PIECES + p] = 1.0
            ones[0, 3 * LANE + SLOT * h + p] = 1.0
    return jnp.asarray(pm, BF16), jnp.asarray(ones, F32)


def _proj_kernel(x_ref, g1_ref, w_ref, wf_ref, bf_ref, sc_ref, gm_ref, pm_ref, ones_ref, avec_ref, c0_ref,
                 ka_ref, va_ref, kb_ref, vb_ref, lf_ref, cum_ref, k_ref, e_ref, qt_ref, vt_ref, ft_ref,
                 carry_ref, *, tiles_per_seq, pos0, w_a, w_b, h_b):
    i = pl.program_id(0)
    tr = x_ref.shape[0]
    seq_tile = i % tiles_per_seq

    x = x_ref[...]
    ms = jnp.mean(x * x, axis=-1, keepdims=True)
    h = (x * lax.rsqrt(ms + RMS_EPS) * g1_ref[...]).astype(BF16)

    def seg(off, width):
        return jnp.dot(h, w_ref[:, off:off + width], preferred_element_type=F32)

    def headnorm(p, scale_row):
        outs = []
        for c in range(0, p.shape[1], MXU_DIM):
            pc = p[:, c:c + MXU_DIM]
            ssq = jnp.dot((pc * pc).astype(BF16), gm_ref[...], preferred_element_type=F32)
            outs.append(pc * lax.rsqrt(ssq * (1.0 / HEAD_DIM) + RMS_EPS) * scale_row[:, c:c + MXU_DIM])
        return jnp.concatenate(outs, axis=1)

    qa = headnorm(seg(0, w_a), sc_ref[0:1, :])
    qt_ref[0:w_a, :] = qa.T.astype(BF16)
    ka = headnorm(seg(w_a, w_a), sc_ref[1:2, :])
    ka_ref[...] = ka
    k_ref[:, 0:w_a] = ka.astype(BF16)
    va = seg(2 * w_a, w_a)
    va_ref[...] = va
    vt_ref[0:w_a, :] = va.T.astype(BF16)
    off = 3 * w_a
    qb = headnorm(seg(off, w_b), sc_ref[2:3, :])
    qt_ref[w_a:w_a + w_b, :] = qb.T.astype(BF16)
    kb = headnorm(seg(off + w_b, w_b), sc_ref[3:4, :])
    kb_ref[...] = kb
    k_ref[:, w_a:w_a + w_b] = kb.astype(BF16)
    vb = seg(off + 2 * w_b, w_b)
    vb_ref[...] = vb
    vt_ref[w_a:w_a + w_b, :] = vb.T.astype(BF16)

    logit = jnp.dot(h, wf_ref[...], preferred_element_type=F32) + bf_ref[...]
    lf = jax.nn.log_sigmoid(logit)
    lf_ref[...] = lf

    @pl.when(seq_tile == 0)
    def _():
        carry_ref[...] = c0_ref[...]

    row = lax.broadcasted_iota(jnp.int32, lf.shape, 0)
    lane = lax.broadcasted_iota(jnp.int32, lf.shape, 1)
    c = lf
    k = 1
    while k < tr:
        c = c + jnp.where(row >= k, pltpu.roll(c, k, axis=0), 0.0)
        k *= 2
    c = c + carry_ref[...]
    carry_ref[...] = c[tr - 1:tr, :]
    cum_ref[...] = c

    pos = (row + (pos0 + seq_tile * tr)).astype(F32)
    z = jnp.where(lane < h_b, c * LOG2E, avec_ref[...] * pos)
    pieces = []
    for _ in range(N_PIECES):
        zp = z.astype(BF16)
        pieces.append(zp)
        z = z - zp.astype(F32)
    ef = jnp.dot(jnp.concatenate(pieces, axis=1), pm_ref[...], preferred_element_type=F32) + ones_ref[...]
    e_ref[...] = ef[:, :2 * LANE].astype(BF16)
    ft_ref[...] = ef[:, 2 * LANE:].T.astype(BF16)


def _proj(x, g1, w_main, w_f, b_f, scales, gmat, pmat, ones_row, avec, c0, *, tr, tiles_per_seq, pos0, w_a, w_b,
          h_b):
    r, d = x.shape
    n = r // tr
    wq = w_a + w_b
    sds = jax.ShapeDtypeStruct
    rows = lambda w: pl.BlockSpec((tr, w), lambda i: (i, 0))
    cols = lambda w: pl.BlockSpec((w, tr), lambda i: (0, i))
    kern = functools.partial(_proj_kernel, tiles_per_seq=tiles_per_seq, pos0=pos0, w_a=w_a, w_b=w_b, h_b=h_b)
    return pl.pallas_call(
        kern,
        grid=(n,),
        in_specs=[rows(d), _const_spec(g1.shape), _const_spec(w_main.shape), _const_spec(w_f.shape),
                  _const_spec(b_f.shape), _const_spec(scales.shape), _const_spec(gmat.shape),
                  _const_spec(pmat.shape), _const_spec(ones_row.shape), _const_spec(avec.shape),
                  _const_spec(c0.shape)],
        out_specs=[rows(w_a), rows(w_a), rows(w_b), rows(w_b), rows(F_LANES), rows(F_LANES),
                   rows(wq), rows(2 * LANE), cols(wq), cols(wq), cols(2 * LANE)],
        out_shape=[sds((r, w_a), F32), sds((r, w_a), F32), sds((r, w_b), F32), sds((r, w_b), F32),
                   sds((r, F_LANES), F32), sds((r, F_LANES), F32),
                   sds((r, wq), BF16), sds((r, 2 * LANE), BF16), sds((wq, r), BF16), sds((wq, r), BF16),
                   sds((2 * LANE, r), BF16)],
        scratch_shapes=[pltpu.VMEM((1, F_LANES), F32)],
        compiler_params=_params(1),
        name="proj",
    )(x, g1, w_main, w_f, b_f, scales, gmat, pmat, ones_row, avec, c0)


def _head_maps(h_a, h_b):
    w_a = h_a * 2 * HEAD_DIM
    maps = []
    for r in range(2 * h_a):
        h, m = divmod(r, 2)
        maps.append(dict(q_row=r * HEAD_DIM, half=m, k_col=h * LANE, e_col=0, f_row=SLOT * h,
                         v_row=h * LANE, v_dim=2 * HEAD_DIM, group=0, acc_row=r * LANE))
    for h in range(h_b):
        maps.append(dict(q_row=w_a + h * HEAD_DIM, half=h % 2, k_col=w_a + (h // 2) * LANE, e_col=LANE,
                         f_row=LANE + SLOT * h, v_row=w_a + h * HEAD_DIM, v_dim=HEAD_DIM, group=1,
                         acc_row=h * HEAD_DIM))
    return maps


def _attn_init(qt_ref, ft_ref, qs_ref, m_ref, l_ref, acca_ref, accb_ref, maps):
    m_ref[...] = jnp.full(m_ref.shape, NEG_INF, F32)
    l_ref[...] = jnp.zeros(l_ref.shape, F32)
    acca_ref[...] = jnp.zeros(acca_ref.shape, F32)
    accb_ref[...] = jnp.zeros(accb_ref.shape, F32)
    qs_ref[...] = jnp.zeros(qs_ref.shape, qs_ref.dtype)
    for r, mp in enumerate(maps):
        base = r * 2 * LANE
        q_dst = base + mp["half"] * HEAD_DIM
        qs_ref[q_dst:q_dst + HEAD_DIM, :] = qt_ref[mp["q_row"]:mp["q_row"] + HEAD_DIM, :]
        f_dst = base + LANE + mp["f_row"] % LANE
        qs_ref[f_dst:f_dst + SLOT, :] = ft_ref[mp["f_row"]:mp["f_row"] + SLOT, :]


def _attn_tile(k_ref, e_ref, vt_ref, mask, qs_ref, m_ref, l_ref, acca_ref, accb_ref, maps, hook=None):
    def scores(r):
        mp = maps[r]
        kaug = jnp.concatenate([k_ref[:, mp["k_col"]:mp["k_col"] + LANE],
                                e_ref[:, mp["e_col"]:mp["e_col"] + LANE]], axis=1)
        return jnp.dot(kaug, qs_ref[r * 2 * LANE:(r + 1) * 2 * LANE, :], preferred_element_type=F32)

    pending = [scores(r) for r in range(min(SCORE_LOOKAHEAD, len(maps)))]
    for r, mp in enumerate(maps):
        if hook is not None and r % 2 == 0:
            hook(r // 2)
        s = pending.pop(0)
        if r + SCORE_LOOKAHEAD < len(maps):
            pending.append(scores(r + SCORE_LOOKAHEAD))
        if mask is not None:
            s = jnp.where(mask, s, NEG_INF)
        tk, tq = s.shape
        m_old = m_ref[r]
        m_new = jnp.maximum(m_old, jnp.max(s, axis=0, keepdims=True))
        alpha = jnp.exp2(m_old - m_new)
        p3 = jnp.exp2(s.reshape(tk // SUBLANES, SUBLANES, tq) - m_new[None])
        l_ref[r] = alpha * l_ref[r] + jnp.sum(p3, axis=0)
        m_ref[r] = m_new
        acc_ref = accb_ref if mp["group"] else acca_ref
        a0, d = mp["acc_row"], mp["v_dim"]
        pv = jnp.dot(vt_ref[mp["v_row"]:mp["v_row"] + d, :], p3.reshape(tk, tq).astype(BF16),
                     preferred_element_type=F32)
        acc3 = acc_ref[a0:a0 + d, :].reshape(d // SUBLANES, SUBLANES, tq)
        acc_ref[a0:a0 + d, :] = (alpha[None] * acc3).reshape(d, tq) + pv


def _diff_lambda(lq1_ref, lk1_ref, lq2_ref, lk2_ref, lam_init):
    return (jnp.exp(jnp.sum(lq1_ref[...] * lk1_ref[...], axis=1, keepdims=True))
            - jnp.exp(jnp.sum(lq2_ref[...] * lk2_ref[...], axis=1, keepdims=True)) + lam_init)


def _attn_finalize(o_ref, osc_ref, lam, l_ref, acca_ref, accb_ref, h_a, h_b):
    inv_l = 1.0 / jnp.sum(l_ref[...], axis=1)
    outs = []
    for h in range(h_a):
        r1, r2 = 2 * h, 2 * h + 1
        o = (acca_ref[r1 * LANE:(r1 + 1) * LANE, :] * inv_l[r1:r1 + 1, :]
             - lam * (acca_ref[r2 * LANE:(r2 + 1) * LANE, :] * inv_l[r2:r2 + 1, :]))
        ms = jnp.mean(o * o, axis=0, keepdims=True)
        outs.append(o * lax.rsqrt(ms + RMS_EPS))
    for h in range(h_b):
        r = 2 * h_a + h
        outs.append(accb_ref[h * HEAD_DIM:(h + 1) * HEAD_DIM, :] * inv_l[r:r + 1, :])
    ot = jnp.concatenate(outs, axis=0)
    o_ref[...] = (ot.T * osc_ref[...]).astype(o_ref.dtype)


def _decode_consts(h_a, h_b, page):
    n_rows, v_a, w_b, a_keys = 2 * h_a, 2 * HEAD_DIM, h_b * HEAD_DIM, page * h_a
    row_a = lax.broadcasted_iota(jnp.int32, (n_rows, v_a), 0)
    lane_a = lax.broadcasted_iota(jnp.int32, (n_rows, v_a), 1)
    row_b = lax.broadcasted_iota(jnp.int32, (h_b, w_b), 0)
    row_s = lax.broadcasted_iota(jnp.int32, (n_rows, a_keys), 0)
    lane_s = lax.broadcasted_iota(jnp.int32, (n_rows, a_keys), 1)
    slope = jnp.zeros((n_rows, a_keys), F32)
    for h, sl in enumerate(_alibi_slopes(h_a)):
        slope = jnp.where(row_s // 2 == h, sl * LOG2E, slope)
    return dict(
        row_a=row_a,
        half_sel=(lane_a // HEAD_DIM) == (row_a % 2),
        own_b=lax.broadcasted_iota(jnp.int32, (h_b, w_b), 1) // HEAD_DIM == row_b,
        valid_a=(lane_s % h_a) == (row_s // 2),
        key_a=lane_s // h_a, slope=slope,
        lane_b=lax.broadcasted_iota(jnp.int32, (h_b, page), 1))


def _decode_chunk(c, slot, dec_in, dec_out, dec_state, pbuf, lfbuf, lam, cst, *, n_pages, page, h_a, h_b):
    qa_ref, qb_ref, kna_ref, knb_ref, vna_ref, vnb_ref, lfn_ref, sub_ref = dec_in
    oa_ref, ob_ref = dec_out
    ma_ref, la_ref, mb_ref, lb_ref, acca_ref, accb_ref, carry_ref = dec_state
    chunks_per_seq = n_pages // DEC_CHUNK
    cs = c % chunks_per_seq
    first = cs == 0
    n_rows, v_a, w_b = 2 * h_a, 2 * HEAD_DIM, h_b * HEAD_DIM
    past = n_pages * page
    row_a = cst["row_a"]
    bcast = lambda col: jnp.broadcast_to(col, (col.shape[0], LANE))

    def per_head_rows(x):
        out = jnp.zeros((n_rows, v_a), F32)
        for h in range(h_a):
            out = jnp.where(row_a // 2 == h, x[h:h + 1, :], out)
        return out

    q8 = jnp.where(cst["half_sel"], per_head_rows(qa_ref[...]), 0.0)
    qblk = jnp.where(cst["own_b"], qb_ref[...], 0.0)
    m_a = jnp.where(first, bcast(jnp.sum(q8 * per_head_rows(kna_ref[...]), axis=1, keepdims=True)), ma_ref[...])
    m_b = jnp.where(first, bcast(jnp.sum(qblk * knb_ref[...], axis=1, keepdims=True)), mb_ref[...])
    l_a = jnp.where(first, 1.0, la_ref[...])
    l_b = jnp.where(first, 1.0, lb_ref[...])
    acc_a = jnp.where(first, per_head_rows(vna_ref[...]), acca_ref[...])
    acc_b = jnp.where(first, jnp.broadcast_to(vnb_ref[...], (h_b, w_b)), accb_ref[...])
    carry = jnp.where(first, bcast(lfn_ref[...]), carry_ref[...])

    contract_last = (((1,), (1,)), ((), ()))
    qa8, qb8 = q8.astype(BF16), qblk.astype(BF16)
    sa_parts, sb_parts = [], []
    for i in range(DEC_CHUNK):
        pidx = n_pages - 1 - (cs * DEC_CHUNK + i)
        sa = lax.dot_general(qa8, pbuf[slot, 0, i].astype(BF16), contract_last, preferred_element_type=F32)
        dist = (past - pidx * page - cst["key_a"]).astype(F32)
        sa_parts.append(jnp.where(cst["valid_a"], sa - cst["slope"] * dist, NEG_INF))
        sb = jnp.dot(qb8, pbuf[slot, 2, i].astype(BF16), preferred_element_type=F32)
        lt = lfbuf[slot, i]
        x = lt
        k = 1
        while k < page:
            x = x + jnp.where(cst["lane_b"] + k < page, pltpu.roll(x, page - k, axis=1), 0.0)
            k *= 2
        sb_parts.append(sb + (carry + x - lt) * LOG2E)
        carry = carry + bcast(x[:, 0:1])
    carry_ref[...] = carry

    def update(s_parts, m_old_b, l_old_b, m_ref, l_ref):
        s = jnp.concatenate(s_parts, axis=1)
        m_old = m_old_b[:, 0:1]
        m_new = jnp.maximum(m_old, jnp.max(s, axis=1, keepdims=True))
        alpha = jnp.exp2(m_old - m_new)
        p = jnp.exp2(s - m_new)
        l_new = alpha * l_old_b[:, 0:1] + jnp.sum(p, axis=1, keepdims=True)
        l_ref[...] = bcast(l_new)
        m_ref[...] = bcast(m_new)
        return alpha, p.astype(BF16), l_new

    alpha_a, pa, l_a = update(sa_parts, m_a, l_a, ma_ref, la_ref)
    va_all = jnp.concatenate([pbuf[slot, 1, i].astype(BF16) for i in range(DEC_CHUNK)], axis=0)
    acc_a = alpha_a * acc_a + jnp.dot(pa, va_all, preferred_element_type=F32)
    acca_ref[...] = acc_a
    alpha_b, pb, l_b = update(sb_parts, m_b, l_b, mb_ref, lb_ref)
    vb_all = jnp.concatenate([pbuf[slot, 3, i].astype(BF16) for i in range(DEC_CHUNK)], axis=1)
    acc_b = alpha_b * acc_b + lax.dot_general(pb, vb_all, contract_last, preferred_element_type=F32)
    accb_ref[...] = acc_b

    inv_a = 1.0 / l_a
    w = acc_a * jnp.where(row_a[:, 0:1] % 2 == 0, inv_a, -lam * inv_a)
    for h in range(h_a):
        o = jnp.sum(jnp.where(row_a // 2 == h, w, 0.0), axis=0, keepdims=True)
        ms = jnp.mean(o * o, axis=1, keepdims=True)
        oa_ref[h:h + 1, :] = o * lax.rsqrt(ms + RMS_EPS) * sub_ref[...]
    ob_ref[...] = jnp.sum(jnp.where(cst["own_b"], acc_b * (1.0 / l_b), 0.0), axis=0, keepdims=True)


def _attn_decode_kernel(qtab_ref, ktab_ref, pt_ref, qt_ref, ft_ref, k_ref, e_ref, vt_ref, km_ref, em_ref, vtm_ref,
                        lq1_ref, lk1_ref, lq2_ref, lk2_ref, osc_ref,
                        dqa_ref, dqb_ref, dkna_ref, dknb_ref, dvna_ref, dvnb_ref, dlfn_ref, sub_ref,
                        ak_hbm, av_hbm, bk_hbm, bv_hbm, lf_hbm,
                        o_ref, oa_ref, ob_ref,
                        qs_ref, m_ref, l_ref, acca_ref, accb_ref,
                        pbuf, lfbuf, sem, dma_ref, dla_ref, dmb_ref, dlb_ref, dacca_ref, daccb_ref, dcarry_ref,
                        *, h_a, h_b, lam_init, layer, n_pages, n_chunks, steps_per_batch):
    t = pl.program_id(1)
    g = pl.program_id(0) * steps_per_batch + t
    qi = qtab_ref[t]
    ki = ktab_ref[t]
    maps = _head_maps(h_a, h_b)
    state = (qs_ref, m_ref, l_ref, acca_ref, accb_ref)
    tk, tq = k_ref.shape[0], qt_ref.shape[1]
    page = lfbuf.shape[3]
    chunks_per_step = len(maps) // 2
    chunks_per_seq = n_pages // DEC_CHUNK
    n_dec_steps = n_chunks // chunks_per_step

    def page_copies(c, slot):
        smp = c // chunks_per_seq
        cs = c % chunks_per_seq
        copies = []
        for i in range(DEC_CHUNK):
            pg = pt_ref[smp, n_pages - 1 - (cs * DEC_CHUNK + i)]
            for a, hbm in enumerate((ak_hbm, av_hbm, bk_hbm, bv_hbm)):
                copies.append(pltpu.make_async_copy(hbm.at[layer, pg], pbuf.at[slot, a, i], sem.at[slot]))
            copies.append(pltpu.make_async_copy(lf_hbm.at[layer, pg], lfbuf.at[slot, i], sem.at[slot]))
        return copies

    @pl.when(g == 0)
    def _():
        for c in range(min(DEC_SLOTS - 1, n_chunks)):
            for cp in page_copies(c, c):
                cp.start()

    @pl.when(ki == 0)
    def _():
        _attn_init(qt_ref, ft_ref, *state, maps)
        key = lax.broadcasted_iota(jnp.int32, (km_ref.shape[0], tq), 0)
        _attn_tile(km_ref, em_ref, vtm_ref, key < N_META, *state, maps)

    lam = _diff_lambda(lq1_ref, lk1_ref, lq2_ref, lk2_ref, lam_init)
    key = lax.broadcasted_iota(jnp.int32, (tk, tq), 0)
    qry = lax.broadcasted_iota(jnp.int32, (tk, tq), 1)
    mask = (key - qry) <= jnp.where(ki < qi, tk, 0)

    dec_in = (dqa_ref, dqb_ref, dkna_ref, dknb_ref, dvna_ref, dvnb_ref, dlfn_ref, sub_ref)
    dec_state = (dma_ref, dla_ref, dmb_ref, dlb_ref, dacca_ref, daccb_ref, dcarry_ref)

    @pl.when(g < n_dec_steps)
    def _():
        cst = _decode_consts(h_a, h_b, page)

        def hook(j):
            c = g * chunks_per_step + j
            for cp in page_copies(c, j % DEC_SLOTS):
                cp.wait()
            ahead = c + DEC_SLOTS - 1

            @pl.when(ahead < n_chunks)
            def _():
                for cp in page_copies(ahead, (j + DEC_SLOTS - 1) % DEC_SLOTS):
                    cp.start()

            _decode_chunk(c, j % DEC_SLOTS, dec_in, (oa_ref, ob_ref), dec_state, pbuf, lfbuf, lam, cst,
                          n_pages=n_pages, page=page, h_a=h_a, h_b=h_b)

        _attn_tile(k_ref, e_ref, vt_ref, mask, *state, maps, hook=hook)

    @pl.when(g >= n_dec_steps)
    def _():
        _attn_tile(k_ref, e_ref, vt_ref, mask, *state, maps)

    @pl.when(ki == qi)
    def _():
        _attn_finalize(o_ref, osc_ref, lam, l_ref, acca_ref, accb_ref, h_a, h_b)


def _attn_meta_kernel(qt_ref, ft_ref, km_ref, em_ref, vtm_ref, lq1_ref, lk1_ref, lq2_ref, lk2_ref, osc_ref, o_ref,
                      qs_ref, m_ref, l_ref, acca_ref, accb_ref, *, h_a, h_b, lam_init):
    maps = _head_maps(h_a, h_b)
    state = (qs_ref, m_ref, l_ref, acca_ref, accb_ref)
    tk, tq = km_ref.shape[0], qt_ref.shape[1]
    _attn_init(qt_ref, ft_ref, *state, maps)
    key = lax.broadcasted_iota(jnp.int32, (tk, tq), 0)
    qry = lax.broadcasted_iota(jnp.int32, (tk, tq), 1)
    _attn_tile(km_ref, em_ref, vtm_ref, (key <= qry) & (key < N_META), *state, maps)
    lam = _diff_lambda(lq1_ref, lk1_ref, lq2_ref, lk2_ref, lam_init)
    _attn_finalize(o_ref, osc_ref, lam, l_ref, acca_ref, accb_ref, h_a, h_b)


def _attn_scratch(tq, h_a, h_b):
    n_maps = 2 * h_a + h_b
    return [pltpu.VMEM((n_maps * 2 * LANE, tq), BF16),
            pltpu.VMEM((n_maps, SUBLANES, tq), F32), pltpu.VMEM((n_maps, SUBLANES, tq), F32),
            pltpu.VMEM((2 * h_a * LANE, tq), F32), pltpu.VMEM((h_b * HEAD_DIM, tq), F32)]


def _attn_decode(qt, ft, k, e, vt, km, em, vtm, lams, osc, page_table, dec_small, sub, caches, *, batch, tile,
                 h_a, h_b, lam_init, layer):
    wq, r = qt.shape
    n = r // batch // tile
    db, n_pages = page_table.shape
    page = caches[4].shape[3]
    w_b = h_b * HEAD_DIM
    v_a = 2 * HEAD_DIM
    n_rows = 2 * h_a
    pairs = [(qi, ki) for qi in range(n) for ki in range(qi + 1)]
    q_tab = jnp.asarray([p[0] for p in pairs], jnp.int32)
    k_tab = jnp.asarray([p[1] for p in pairs], jnp.int32)
    n_chunks = db * n_pages // DEC_CHUNK
    chunks_per_step = (2 * h_a + h_b) // 2
    chunks_per_seq = n_pages // DEC_CHUNK
    assert n_pages % DEC_CHUNK == 0 and chunks_per_seq % chunks_per_step == 0 and chunks_per_step % DEC_SLOTS == 0
    steps_per_seq = chunks_per_seq // chunks_per_step
    assert n_chunks // chunks_per_step <= batch * len(pairs)
    q_blk = lambda b, t, qtab, ktab, pt: b * n + qtab[t]
    k_blk = lambda b, t, qtab, ktab, pt: b * n + ktab[t]
    hbm = pl.BlockSpec(memory_space=pl.ANY)
    seq_of = lambda b, t, qtab, ktab, pt: jnp.minimum((b * len(pairs) + t) // steps_per_seq, db - 1)
    per_seq = lambda a: pl.BlockSpec((None,) + a.shape[1:], lambda *i: (seq_of(*i),) + (0,) * (a.ndim - 1))
    in_specs = [
        pl.BlockSpec((wq, tile), lambda *a: (0, q_blk(*a))),
        pl.BlockSpec((2 * LANE, tile), lambda *a: (0, q_blk(*a))),
        pl.BlockSpec((tile, wq), lambda *a: (k_blk(*a), 0)),
        pl.BlockSpec((tile, 2 * LANE), lambda *a: (k_blk(*a), 0)),
        pl.BlockSpec((wq, tile), lambda *a: (0, k_blk(*a))),
        _const_spec(km.shape), _const_spec(em.shape), _const_spec(vtm.shape),
    ] + [_const_spec(a.shape) for a in (*lams, osc)] + [per_seq(a) for a in dec_small] + [_const_spec(sub.shape)] \
        + [hbm] * 5
    kern = functools.partial(_attn_decode_kernel, h_a=h_a, h_b=h_b, lam_init=lam_init, layer=layer,
                             n_pages=n_pages, n_chunks=n_chunks, steps_per_batch=len(pairs))
    grid_spec = pltpu.PrefetchScalarGridSpec(
        num_scalar_prefetch=3,
        grid=(batch, len(pairs)),
        in_specs=in_specs,
        out_specs=[pl.BlockSpec((tile, wq), lambda *a: (q_blk(*a), 0)),
                   pl.BlockSpec((None, h_a, v_a), lambda *i: (seq_of(*i), 0, 0)),
                   pl.BlockSpec((None, 1, w_b), lambda *i: (seq_of(*i), 0, 0))],
        scratch_shapes=_attn_scratch(tile, h_a, h_b) + [
            pltpu.VMEM((DEC_SLOTS, 4, DEC_CHUNK, page * h_a, v_a), F32),
            pltpu.VMEM((DEC_SLOTS, DEC_CHUNK, h_b, page), F32),
            pltpu.SemaphoreType.DMA((DEC_SLOTS,)),
            pltpu.VMEM((n_rows, LANE), F32), pltpu.VMEM((n_rows, LANE), F32),
            pltpu.VMEM((h_b, LANE), F32), pltpu.VMEM((h_b, LANE), F32),
            pltpu.VMEM((n_rows, v_a), F32), pltpu.VMEM((h_b, w_b), F32), pltpu.VMEM((h_b, LANE), F32)],
    )
    return pl.pallas_call(
        kern,
        grid_spec=grid_spec,
        out_shape=[jax.ShapeDtypeStruct((r, wq), BF16), jax.ShapeDtypeStruct((db, h_a, v_a), F32),
                   jax.ShapeDtypeStruct((db, 1, w_b), F32)],
        compiler_params=_params(2),
        name="attn_decode",
    )(q_tab, k_tab, page_table, qt, ft, k, e, vt, km, em, vtm, *lams, osc, *dec_small, sub, *caches)


def _attn_meta(qt, ft, km, em, vtm, lams, osc, *, h_a, h_b, lam_init):
    wq, r = qt.shape
    args = (qt, ft, km, em, vtm, *lams, osc)
    kern = functools.partial(_attn_meta_kernel, h_a=h_a, h_b=h_b, lam_init=lam_init)
    return pl.pallas_call(
        kern,
        grid=(1,),
        in_specs=[_const_spec(a.shape) for a in args],
        out_specs=_const_spec((r, wq)),
        out_shape=jax.ShapeDtypeStruct((r, wq), BF16),
        scratch_shapes=_attn_scratch(r, h_a, h_b),
        compiler_params=_params(1),
        name="attn_meta",
    )(*args)


def _finish_kernel(x_ref, o_ref, wo_ref, g2_ref, wup_ref, wdn_ref, y_ref):
    x1 = x_ref[...] + jnp.dot(o_ref[...], wo_ref[...], preferred_element_type=F32)
    ms = jnp.mean(x1 * x1, axis=-1, keepdims=True)
    hn = (x1 * lax.rsqrt(ms + RMS_EPS) * g2_ref[...]).astype(BF16)
    acc = x1
    for c in range(0, wup_ref.shape[1], FF_CHUNK):
        u = jnp.maximum(jnp.dot(hn, wup_ref[:, c:c + FF_CHUNK], preferred_element_type=F32), 0.0)
        acc = acc + jnp.dot((u * u).astype(BF16), wdn_ref[c:c + FF_CHUNK, :], preferred_element_type=F32)
    y_ref[...] = acc


def _finish(x, o, w_o, g2, w_up, w_down, *, tr):
    r, d = x.shape
    rows = lambda w: pl.BlockSpec((tr, w), lambda i: (i, 0))
    return pl.pallas_call(
        _finish_kernel,
        grid=(r // tr,),
        in_specs=[rows(d), rows(o.shape[1]), _const_spec(w_o.shape), _const_spec(g2.shape),
                  _const_spec(w_up.shape), _const_spec(w_down.shape)],
        out_specs=rows(d),
        out_shape=jax.ShapeDtypeStruct((r, d), F32),
        compiler_params=_params(1),
        name="finish",
    )(x, o, w_o, g2, w_up, w_down)


def kernel(x_prompt, x_sample, cache_a_k, cache_a_v, cache_b_k, cache_b_v, cache_b_logf, page_table, meta_tokens,
           norm1, w_in, b_f, qn_a, kn_a, qn_b, kn_b, lam_q1, lam_k1, lam_q2, lam_k2, subln, w_o, norm2, w_up,
           w_down):
    batch, seq, d = x_prompt.shape
    db, dec_seq, _ = x_sample.shape
    depth, _, page, h_a, v_a = cache_a_k.shape
    h_b = cache_b_k.shape[3]
    w_a, w_b = h_a * v_a, h_b * HEAD_DIM
    wq = w_a + w_b
    n_small = LANE
    assert dec_seq == 1 and v_a == 2 * HEAD_DIM and meta_tokens.shape[0] == N_META
    assert w_a == w_b and w_a % MXU_DIM == 0 and h_a + h_b <= F_LANES
    assert SLOT * h_a <= LANE and SLOT * h_b <= LANE and N_META + db <= n_small
    assert seq % ATTN_TILE == 0 and seq % ROW_TILE == 0
    assert w_in.shape[2] == 3 * wq + h_b

    gmat = _group_sum_matrix()
    pmat, ones_row = _bias_placement(h_a, h_b)
    avec = np.zeros((1, F_LANES), np.float32)
    avec[0, h_b:h_b + h_a] = [s * LOG2E for s in _alibi_slopes(h_a)]
    avec = jnp.asarray(avec)
    q_scale = HEAD_DIM ** -0.5 * LOG2E

    n_pool = cache_a_k.shape[1]
    ak2 = cache_a_k.reshape(depth, n_pool, page * h_a, v_a)
    av2 = cache_a_v.reshape(depth, n_pool, page * h_a, v_a)
    bkt = cache_b_k.transpose(0, 1, 3, 4, 2).reshape(depth, n_pool, w_b, page)
    bvt = cache_b_v.transpose(0, 1, 3, 4, 2).reshape(depth, n_pool, w_b, page)
    lft = cache_b_logf.transpose(0, 1, 3, 2)

    x_main = x_prompt.reshape(batch * seq, d)
    x_small = jnp.concatenate([meta_tokens.astype(F32), x_sample.reshape(db, d),
                               jnp.zeros((n_small - N_META - db, d), F32)], axis=0)
    samples = slice(N_META, N_META + db)
    zero_c0 = jnp.zeros((1, F_LANES), F32)

    outs = {name: [] for name in ("p_ak", "p_av", "p_bk", "p_bv", "p_lf", "s_ak", "s_av", "s_bk", "s_bv", "s_lf")}
    for l in range(depth):
        lam_init = _lambda_init(l)
        w_main = w_in[l, :, :3 * wq].astype(BF16)
        w_f = jnp.pad(w_in[l, :, 3 * wq:], ((0, 0), (0, F_LANES - h_b))).astype(BF16)
        bias_f = jnp.pad(b_f[l], (0, F_LANES - h_b)).reshape(1, F_LANES)
        g1 = norm1[l].reshape(1, d)
        scales = jnp.stack([jnp.tile(qn_a[l], w_a // HEAD_DIM) * q_scale, jnp.tile(kn_a[l], w_a // HEAD_DIM),
                            jnp.tile(qn_b[l], w_b // HEAD_DIM) * q_scale, jnp.tile(kn_b[l], w_b // HEAD_DIM)])
        lams = tuple(a[l].reshape(1, HEAD_DIM) for a in (lam_q1, lam_k1, lam_q2, lam_k2))
        sub_row = (subln[l] * (1.0 - lam_init)).reshape(1, v_a)
        osc = jnp.concatenate([jnp.tile(sub_row, (1, h_a)), jnp.ones((1, w_b), F32)], axis=1)
        proj = functools.partial(_proj, g1=g1, w_main=w_main, w_f=w_f, b_f=bias_f, scales=scales, gmat=gmat,
                                 pmat=pmat, ones_row=ones_row, avec=avec, w_a=w_a, w_b=w_b, h_b=h_b)

        (ka_s, va_s, kb_s, vb_s, lf_s, cum_s, k_s, e_s, qt_s, vt_s, ft_s) = proj(
            x_small, c0=zero_c0, tr=n_small, tiles_per_seq=1, pos0=0)
        (ka_m, va_m, kb_m, vb_m, lf_m, _, k_m, e_m, qt_m, vt_m, ft_m) = proj(
            x_main, c0=cum_s[N_META - 1:N_META], tr=ROW_TILE, tiles_per_seq=seq // ROW_TILE, pos0=N_META)

        o_meta = _attn_meta(qt_s, ft_s, k_s, e_s, vt_s, lams, osc, h_a=h_a, h_b=h_b, lam_init=lam_init)
        q_rows = qt_s[:, samples].T.astype(F32)
        dec_small = (q_rows[:, :w_a].reshape(db, h_a, v_a), q_rows[:, w_a:].reshape(db, 1, w_b),
                     ka_s[samples].reshape(db, h_a, v_a), kb_s[samples].reshape(db, 1, w_b),
                     va_s[samples].reshape(db, h_a, v_a), vb_s[samples].reshape(db, 1, w_b),
                     lf_s[samples, :h_b].reshape(db, h_b, 1))
        o_main, oa_s, ob_s = _attn_decode(qt_m, ft_m, k_m, e_m, vt_m, k_s, e_s, vt_s, lams, osc, page_table,
                                          dec_small, sub_row, (ak2, av2, bkt, bvt, lft), batch=batch,
                                          tile=ATTN_TILE, h_a=h_a, h_b=h_b, lam_init=lam_init, layer=l)
        o_samples = jnp.concatenate([oa_s.reshape(db, w_a), ob_s.reshape(db, w_b)], axis=1).astype(BF16)
        o_small = jnp.concatenate([o_meta[:N_META], o_samples, jnp.zeros((n_small - N_META - db, wq), BF16)],
                                  axis=0)

        fin = functools.partial(_finish, w_o=w_o[l].astype(BF16), g2=norm2[l].reshape(1, d),
                                w_up=w_up[l].astype(BF16), w_down=w_down[l].astype(BF16))
        x_small = fin(x_small, o_small, tr=n_small)
        x_main = fin(x_main, o_main, tr=ROW_TILE)

        def prompt_rows(small, main):
            pieces = []
            for b in range(batch):
                pieces += [small[:N_META], main[b * seq:(b + 1) * seq]]
            return pieces

        outs["p_ak"] += prompt_rows(ka_s, ka_m)
        outs["p_av"] += prompt_rows(va_s, va_m)
        outs["p_bk"] += prompt_rows(kb_s, kb_m)
        outs["p_bv"] += prompt_rows(vb_s, vb_m)
        outs["p_lf"] += prompt_rows(lf_s[:, :h_b], lf_m[:, :h_b])
        outs["s_ak"].append(ka_s[samples].reshape(db, 1, h_a, v_a))
        outs["s_av"].append(va_s[samples].reshape(db, 1, h_a, v_a))
        outs["s_bk"].append(kb_s[samples].reshape(db, 1, h_b, HEAD_DIM))
        outs["s_bv"].append(vb_s[samples].reshape(db, 1, h_b, HEAD_DIM))
        outs["s_lf"].append(lf_s[samples, :h_b].reshape(db, 1, h_b))

    y_prompt = x_main.reshape(batch, seq, d)
    y_sample = x_small[samples].reshape(db, 1, d)
    stack = lambda name: jnp.stack(outs[name])
    rows = lambda name, tail: jnp.concatenate(outs[name], axis=0).reshape((depth, batch, N_META + seq) + tail)
    return (y_prompt, y_sample, rows("p_ak", (h_a, v_a)), rows("p_av", (h_a, v_a)), rows("p_bk", (h_b, HEAD_DIM)),
            rows("p_bv", (h_b, HEAD_DIM)), rows("p_lf", (h_b,)),
            stack("s_ak"), stack("s_av"), stack("s_bk"), stack("s_bv"), stack("s_lf"))
```

```python
import functools
import math

import numpy as np
import jax
import jax.numpy as jnp
from jax import lax
from jax.experimental import pallas as pl
from jax.experimental.pallas import tpu as pltpu

F32 = jnp.float32
BF16 = jnp.bfloat16

HEAD_DIM = 64
N_META = 16
RMS_EPS = 1e-6
NEG_INF = -1e30
F_LANES = 128
LOG2E = math.log2(math.e)

LANE = 128
SUBLANES = 8
MXU_DIM = 256
BF16_SUBLANES = 16
VMEM_LIMIT = 56 * 1024 * 1024

SLOT = BF16_SUBLANES
N_PIECES = 3

ROW_TILE = 512
ATTN_TILE = 512
DEC_CHUNK = 8
DEC_SLOTS = 3
DEC_HOOKS = 4
FF_CHUNK = 1024
SCORE_LOOKAHEAD = 2


def _lambda_init(layer):
    return 0.8 - 0.6 * math.exp(-0.3 * layer)


def _alibi_slopes(h_a):
    return [2.0 ** (-8.0 * (i + 1) / h_a) for i in range(h_a)]


def _const_spec(shape):
    return pl.BlockSpec(shape, lambda *_: (0,) * len(shape))


def _params(n_axes):
    return pltpu.CompilerParams(dimension_semantics=("arbitrary",) * n_axes, vmem_limit_bytes=VMEM_LIMIT)


def _group_sum_matrix():
    idx = np.arange(MXU_DIM) // HEAD_DIM
    return jnp.asarray((idx[:, None] == idx[None, :]).astype(np.float32), BF16)


def _bias_placement(h_a, h_b):
    pm = np.zeros((N_PIECES * F_LANES, 4 * LANE), np.float32)
    ones = np.zeros((1, 4 * LANE), np.float32)
    for h in range(h_a):
        for p in range(N_PIECES):
            pm[p * F_LANES + h_b + h, 0 * LANE + SLOT * h + p] = 1.0
            pm[p * F_LANES + h_b + h, 2 * LANE + SLOT * h + N_PIECES + p] = -1.0
            ones[0, 0 * LANE + SLOT * h + N_PIECES + p] = 1.0
            ones[0, 2 * LANE + SLOT * h + p] = 1.0
    for h in range(h_b):
        for p in range(N_PIECES):
            pm[p * F_LANES + h, 1 * LANE + SLOT * h + p] = -1.0
            pm[p * F_LANES + h, 3 * LANE + SLOT * h + N_PIECES + p] = 1.0
            ones[0, 1 * LANE + SLOT * h + N_PIECES + p] = 1.0
            ones[0, 3 * LANE + SLOT * h + p] = 1.0
    return jnp.asarray(pm, BF16), jnp.asarray(ones, F32)


def _proj_kernel(x_ref, g1_ref, w_ref, wf_ref, bf_ref, sc_ref, gm_ref, pm_ref, ones_ref, avec_ref, c0_ref,
                 ka_ref, va_ref, kb_ref, vb_ref, lf_ref, cum_ref, k_ref, e_ref, qt_ref, vt_ref, ft_ref,
                 carry_ref, *, tiles_per_seq, pos0, w_a, w_b, h_b):
    i = pl.program_id(0)
    tr = x_ref.shape[0]
    seq_tile = i % tiles_per_seq

    x = x_ref[...]
    ms = jnp.mean(x * x, axis=-1, keepdims=True)
    h = (x * lax.rsqrt(ms + RMS_EPS) * g1_ref[...]).astype(BF16)

    def seg(off, width):
        return jnp.dot(h, w_ref[:, off:off + width], preferred_element_type=F32)

    def headnorm(p, scale_row):
        outs = []
        for c in range(0, p.shape[1], MXU_DIM):
            pc = p[:, c:c + MXU_DIM]
            ssq = jnp.dot((pc * pc).astype(BF16), gm_ref[...], preferred_element_type=F32)
            outs.append(pc * lax.rsqrt(ssq * (1.0 / HEAD_DIM) + RMS_EPS) * scale_row[:, c:c + MXU_DIM])
        return jnp.concatenate(outs, axis=1)

    qa = headnorm(seg(0, w_a), sc_ref[0:1, :])
    qt_ref[0:w_a, :] = qa.T.astype(BF16)
    ka = headnorm(seg(w_a, w_a), sc_ref[1:2, :])
    ka_ref[...] = ka
    k_ref[:, 0:w_a] = ka.astype(BF16)
    va = seg(2 * w_a, w_a)
    va_ref[...] = va
    vt_ref[0:w_a, :] = va.T.astype(BF16)
    off = 3 * w_a
    qb = headnorm(seg(off, w_b), sc_ref[2:3, :])
    qt_ref[w_a:w_a + w_b, :] = qb.T.astype(BF16)
    kb = headnorm(seg(off + w_b, w_b), sc_ref[3:4, :])
    kb_ref[...] = kb
    k_ref[:, w_a:w_a + w_b] = kb.astype(BF16)
    vb = seg(off + 2 * w_b, w_b)
    vb_ref[...] = vb
    vt_ref[w_a:w_a + w_b, :] = vb.T.astype(BF16)

    logit = jnp.dot(h, wf_ref[...], preferred_element_type=F32) + bf_ref[...]
    lf = jax.nn.log_sigmoid(logit)
    lf_ref[...] = lf

    @pl.when(seq_tile == 0)
    def _():
        carry_ref[...] = c0_ref[...]

    row = lax.broadcasted_iota(jnp.int32, lf.shape, 0)
    lane = lax.broadcasted_iota(jnp.int32, lf.shape, 1)
    c = lf
    k = 1
    while k < tr:
        c = c + jnp.where(row >= k, pltpu.roll(c, k, axis=0), 0.0)
        k *= 2
    c = c + carry_ref[...]
    carry_ref[...] = c[tr - 1:tr, :]
    cum_ref[...] = c

    pos = (row + (pos0 + seq_tile * tr)).astype(F32)
    z = jnp.where(lane < h_b, c * LOG2E, avec_ref[...] * pos)
    pieces = []
    for _ in range(N_PIECES):
        zp = z.astype(BF16)
        pieces.append(zp)
        z = z - zp.astype(F32)
    ef = jnp.dot(jnp.concatenate(pieces, axis=1), pm_ref[...], preferred_element_type=F32) + ones_ref[...]
    e_ref[...] = ef[:, :2 * LANE].astype(BF16)
    ft_ref[...] = ef[:, 2 * LANE:].T.astype(BF16)


def _proj(x, g1, w_main, w_f, b_f, scales, gmat, pmat, ones_row, avec, c0, *, tr, tiles_per_seq, pos0, w_a, w_b,
          h_b):
    r, d = x.shape
    n = r // tr
    wq = w_a + w_b
    sds = jax.ShapeDtypeStruct
    rows = lambda w: pl.BlockSpec((tr, w), lambda i: (i, 0))
    cols = lambda w: pl.BlockSpec((w, tr), lambda i: (0, i))
    kern = functools.partial(_proj_kernel, tiles_per_seq=tiles_per_seq, pos0=pos0, w_a=w_a, w_b=w_b, h_b=h_b)
    return pl.pallas_call(
        kern,
        grid=(n,),
        in_specs=[rows(d), _const_spec(g1.shape), _const_spec(w_main.shape), _const_spec(w_f.shape),
                  _const_spec(b_f.shape), _const_spec(scales.shape), _const_spec(gmat.shape),
                  _const_spec(pmat.shape), _const_spec(ones_row.shape), _const_spec(avec.shape),
                  _const_spec(c0.shape)],
        out_specs=[rows(w_a), rows(w_a), rows(w_b), rows(w_b), rows(F_LANES), rows(F_LANES),
                   rows(wq), rows(2 * LANE), cols(wq), cols(wq), cols(2 * LANE)],
        out_shape=[sds((r, w_a), F32), sds((r, w_a), F32), sds((r, w_b), F32), sds((r, w_b), F32),
                   sds((r, F_LANES), F32), sds((r, F_LANES), F32),
                   sds((r, wq), BF16), sds((r, 2 * LANE), BF16), sds((wq, r), BF16), sds((wq, r), BF16),
                   sds((2 * LANE, r), BF16)],
        scratch_shapes=[pltpu.VMEM((1, F_LANES), F32)],
        compiler_params=_params(1),
        name="proj",
    )(x, g1, w_main, w_f, b_f, scales, gmat, pmat, ones_row, avec, c0)


def _head_maps(h_a, h_b):
    w_a = h_a * 2 * HEAD_DIM
    maps = []
    for r in range(2 * h_a):
        h, m = divmod(r, 2)
        maps.append(dict(q_row=r * HEAD_DIM, half=m, k_col=h * LANE, e_col=0, f_row=SLOT * h,
                         v_row=h * LANE, v_dim=2 * HEAD_DIM, group=0, acc_row=r * LANE))
    for h in range(h_b):
        maps.append(dict(q_row=w_a + h * HEAD_DIM, half=h % 2, k_col=w_a + (h // 2) * LANE, e_col=LANE,
                         f_row=LANE + SLOT * h, v_row=w_a + h * HEAD_DIM, v_dim=HEAD_DIM, group=1,
                         acc_row=h * HEAD_DIM))
    return maps


def _attn_init(qt_ref, ft_ref, qs_ref, m_ref, l_ref, acca_ref, accb_ref, maps):
    m_ref[...] = jnp.full(m_ref.shape, NEG_INF, F32)
    l_ref[...] = jnp.zeros(l_ref.shape, F32)
    acca_ref[...] = jnp.zeros(acca_ref.shape, F32)
    accb_ref[...] = jnp.zeros(accb_ref.shape, F32)
    qs_ref[...] = jnp.zeros(qs_ref.shape, qs_ref.dtype)
    for r, mp in enumerate(maps):
        base = r * 2 * LANE
        q_dst = base + mp["half"] * HEAD_DIM
        qs_ref[q_dst:q_dst + HEAD_DIM, :] = qt_ref[mp["q_row"]:mp["q_row"] + HEAD_DIM, :]
        f_dst = base + LANE + mp["f_row"] % LANE
        qs_ref[f_dst:f_dst + SLOT, :] = ft_ref[mp["f_row"]:mp["f_row"] + SLOT, :]


def _attn_tile(k_ref, e_ref, vt_ref, mask, qs_ref, m_ref, l_ref, acca_ref, accb_ref, maps, hook=None):
    def scores(r):
        mp = maps[r]
        kaug = jnp.concatenate([k_ref[:, mp["k_col"]:mp["k_col"] + LANE],
                                e_ref[:, mp["e_col"]:mp["e_col"] + LANE]], axis=1)
        return jnp.dot(kaug, qs_ref[r * 2 * LANE:(r + 1) * 2 * LANE, :], preferred_element_type=F32)

    pending = [scores(r) for r in range(min(SCORE_LOOKAHEAD, len(maps)))]
    for r, mp in enumerate(maps):
        if hook is not None and r % (len(maps) // DEC_HOOKS) == 0:
            hook(r // (len(maps) // DEC_HOOKS))
        s = pending.pop(0)
        if r + SCORE_LOOKAHEAD < len(maps):
            pending.append(scores(r + SCORE_LOOKAHEAD))
        if mask is not None:
            s = jnp.where(mask, s, NEG_INF)
        tk, tq = s.shape
        m_old = m_ref[r]
        m_new = jnp.maximum(m_old, jnp.max(s, axis=0, keepdims=True))
        alpha = jnp.exp2(m_old - m_new)
        p3 = jnp.exp2(s.reshape(tk // SUBLANES, SUBLANES, tq) - m_new[None])
        l_ref[r] = alpha * l_ref[r] + jnp.sum(p3, axis=0)
        m_ref[r] = m_new
        acc_ref = accb_ref if mp["group"] else acca_ref
        a0, d = mp["acc_row"], mp["v_dim"]
        pv = jnp.dot(vt_ref[mp["v_row"]:mp["v_row"] + d, :], p3.reshape(tk, tq).astype(BF16),
                     preferred_element_type=F32)
        acc3 = acc_ref[a0:a0 + d, :].reshape(d // SUBLANES, SUBLANES, tq)
        acc_ref[a0:a0 + d, :] = (alpha[None] * acc3).reshape(d, tq) + pv


def _diff_lambda(lq1_ref, lk1_ref, lq2_ref, lk2_ref, lam_init):
    return (jnp.exp(jnp.sum(lq1_ref[...] * lk1_ref[...], axis=1, keepdims=True))
            - jnp.exp(jnp.sum(lq2_ref[...] * lk2_ref[...], axis=1, keepdims=True)) + lam_init)


def _attn_finalize(o_ref, osc_ref, lam, l_ref, acca_ref, accb_ref, h_a, h_b):
    inv_l = 1.0 / jnp.sum(l_ref[...], axis=1)
    outs = []
    for h in range(h_a):
        r1, r2 = 2 * h, 2 * h + 1
        o = (acca_ref[r1 * LANE:(r1 + 1) * LANE, :] * inv_l[r1:r1 + 1, :]
             - lam * (acca_ref[r2 * LANE:(r2 + 1) * LANE, :] * inv_l[r2:r2 + 1, :]))
        ms = jnp.mean(o * o, axis=0, keepdims=True)
        outs.append(o * lax.rsqrt(ms + RMS_EPS))
    for h in range(h_b):
        r = 2 * h_a + h
        outs.append(accb_ref[h * HEAD_DIM:(h + 1) * HEAD_DIM, :] * inv_l[r:r + 1, :])
    ot = jnp.concatenate(outs, axis=0)
    o_ref[...] = (ot.T * osc_ref[...]).astype(o_ref.dtype)


def _decode_consts(h_a, h_b, page):
    n_rows, v_a, w_b, a_keys = 2 * h_a, 2 * HEAD_DIM, h_b * HEAD_DIM, page * h_a
    row_a = lax.broadcasted_iota(jnp.int32, (n_rows, v_a), 0)
    lane_a = lax.broadcasted_iota(jnp.int32, (n_rows, v_a), 1)
    row_b = lax.broadcasted_iota(jnp.int32, (h_b, w_b), 0)
    row_s = lax.broadcasted_iota(jnp.int32, (n_rows, a_keys), 0)
    lane_s = lax.broadcasted_iota(jnp.int32, (n_rows, a_keys), 1)
    slope = jnp.zeros((n_rows, a_keys), F32)
    for h, sl in enumerate(_alibi_slopes(h_a)):
        slope = jnp.where(row_s // 2 == h, sl * LOG2E, slope)
    return dict(
        row_a=row_a,
        half_sel=(lane_a // HEAD_DIM) == (row_a % 2),
        own_b=lax.broadcasted_iota(jnp.int32, (h_b, w_b), 1) // HEAD_DIM == row_b,
        valid_a=(lane_s % h_a) == (row_s // 2),
        key_a=lane_s // h_a, slope=slope,
        lane_b=lax.broadcasted_iota(jnp.int32, (h_b, page), 1))


def _decode_chunk(c, slot, dec_in, dec_out, dec_state, pbuf, lfbuf, lam, cst, *, n_pages, page, h_a, h_b):
    qa_ref, qb_ref, kna_ref, knb_ref, vna_ref, vnb_ref, lfn_ref, sub_ref = dec_in
    oa_ref, ob_ref = dec_out
    ma_ref, la_ref, mb_ref, lb_ref, acca_ref, accb_ref, carry_ref = dec_state
    chunks_per_seq = n_pages // DEC_CHUNK
    cs = c % chunks_per_seq
    first = cs == 0
    n_rows, v_a, w_b = 2 * h_a, 2 * HEAD_DIM, h_b * HEAD_DIM
    past = n_pages * page
    row_a = cst["row_a"]
    bcast = lambda col: jnp.broadcast_to(col, (col.shape[0], LANE))

    def per_head_rows(x):
        out = jnp.zeros((n_rows, v_a), F32)
        for h in range(h_a):
            out = jnp.where(row_a // 2 == h, x[h:h + 1, :], out)
        return out

    q8 = jnp.where(cst["half_sel"], per_head_rows(qa_ref[...]), 0.0)
    qblk = jnp.where(cst["own_b"], qb_ref[...], 0.0)
    m_a = jnp.where(first, bcast(jnp.sum(q8 * per_head_rows(kna_ref[...]), axis=1, keepdims=True)), ma_ref[...])
    m_b = jnp.where(first, bcast(jnp.sum(qblk * knb_ref[...], axis=1, keepdims=True)), mb_ref[...])
    l_a = jnp.where(first, 1.0, la_ref[...])
    l_b = jnp.where(first, 1.0, lb_ref[...])
    acc_a = jnp.where(first, per_head_rows(vna_ref[...]), acca_ref[...])
    acc_b = jnp.where(first, jnp.broadcast_to(vnb_ref[...], (h_b, w_b)), accb_ref[...])
    carry = jnp.where(first, bcast(lfn_ref[...]), carry_ref[...])

    contract_last = (((1,), (1,)), ((), ()))
    qa8, qb8 = q8.astype(BF16), qblk.astype(BF16)
    sa_parts, sb_parts = [], []
    for i in range(DEC_CHUNK):
        pidx = n_pages - 1 - (cs * DEC_CHUNK + i)
        sa = lax.dot_general(qa8, pbuf[slot, 0, i].astype(BF16), contract_last, preferred_element_type=F32)
        dist = (past - pidx * page - cst["key_a"]).astype(F32)
        sa_parts.append(jnp.where(cst["valid_a"], sa - cst["slope"] * dist, NEG_INF))
        sb = jnp.dot(qb8, pbuf[slot, 2, i].astype(BF16), preferred_element_type=F32)
        lt = lfbuf[slot, i]
        x = lt
        k = 1
        while k < page:
            x = x + jnp.where(cst["lane_b"] + k < page, pltpu.roll(x, page - k, axis=1), 0.0)
            k *= 2
        sb_parts.append(sb + (carry + x - lt) * LOG2E)
        carry = carry + bcast(x[:, 0:1])
    carry_ref[...] = carry

    def update(s_parts, m_old_b, l_old_b, m_ref, l_ref):
        s = jnp.concatenate(s_parts, axis=1)
        m_old = m_old_b[:, 0:1]
        m_new = jnp.maximum(m_old, jnp.max(s, axis=1, keepdims=True))
        alpha = jnp.exp2(m_old - m_new)
        p = jnp.exp2(s - m_new)
        l_new = alpha * l_old_b[:, 0:1] + jnp.sum(p, axis=1, keepdims=True)
        l_ref[...] = bcast(l_new)
        m_ref[...] = bcast(m_new)
        return alpha, p.astype(BF16), l_new

    alpha_a, pa, l_a = update(sa_parts, m_a, l_a, ma_ref, la_ref)
    va_all = jnp.concatenate([pbuf[slot, 1, i].astype(BF16) for i in range(DEC_CHUNK)], axis=0)
    acc_a = alpha_a * acc_a + jnp.dot(pa, va_all, preferred_element_type=F32)
    acca_ref[...] = acc_a
    alpha_b, pb, l_b = update(sb_parts, m_b, l_b, mb_ref, lb_ref)
    vb_all = jnp.concatenate([pbuf[slot, 3, i].astype(BF16) for i in range(DEC_CHUNK)], axis=1)
    acc_b = alpha_b * acc_b + lax.dot_general(pb, vb_all, contract_last, preferred_element_type=F32)
    accb_ref[...] = acc_b

    inv_a = 1.0 / l_a
    w = acc_a * jnp.where(row_a[:, 0:1] % 2 == 0, inv_a, -lam * inv_a)
    for h in range(h_a):
        o = jnp.sum(jnp.where(row_a // 2 == h, w, 0.0), axis=0, keepdims=True)
        ms = jnp.mean(o * o, axis=1, keepdims=True)
        oa_ref[h:h + 1, :] = o * lax.rsqrt(ms + RMS_EPS) * sub_ref[...]
    ob_ref[...] = jnp.sum(jnp.where(cst["own_b"], acc_b * (1.0 / l_b), 0.0), axis=0, keepdims=True)


def _attn_decode_kernel(qtab_ref, ktab_ref, pt_ref, qt_ref, ft_ref, k_ref, e_ref, vt_ref, km_ref, em_ref, vtm_ref,
                        lq1_ref, lk1_ref, lq2_ref, lk2_ref, osc_ref,
                        dqa_ref, dqb_ref, dkna_ref, dknb_ref, dvna_ref, dvnb_ref, dlfn_ref, sub_ref,
                        ak_hbm, av_hbm, bk_hbm, bv_hbm, lf_hbm,
                        o_ref, oa_ref, ob_ref,
                        qs_ref, m_ref, l_ref, acca_ref, accb_ref,
                        pbuf, lfbuf, sem, dma_ref, dla_ref, dmb_ref, dlb_ref, dacca_ref, daccb_ref, dcarry_ref,
                        *, h_a, h_b, lam_init, layer, n_pages, n_chunks, steps_per_batch):
    t = pl.program_id(1)
    g = pl.program_id(0) * steps_per_batch + t
    qi = qtab_ref[t]
    ki = ktab_ref[t]
    maps = _head_maps(h_a, h_b)
    state = (qs_ref, m_ref, l_ref, acca_ref, accb_ref)
    tk, tq = k_ref.shape[0], qt_ref.shape[1]
    page = lfbuf.shape[3]
    chunks_per_step = DEC_HOOKS
    chunks_per_seq = n_pages // DEC_CHUNK
    n_dec_steps = n_chunks // chunks_per_step

    def page_copies(c, slot):
        smp = c // chunks_per_seq
        cs = c % chunks_per_seq
        copies = []
        for i in range(DEC_CHUNK):
            pg = pt_ref[smp, n_pages - 1 - (cs * DEC_CHUNK + i)]
            for a, hbm in enumerate((ak_hbm, av_hbm, bk_hbm, bv_hbm)):
                copies.append(pltpu.make_async_copy(hbm.at[layer, pg], pbuf.at[slot, a, i], sem.at[slot]))
            copies.append(pltpu.make_async_copy(lf_hbm.at[layer, pg], lfbuf.at[slot, i], sem.at[slot]))
        return copies

    @pl.when(g == 0)
    def _():
        for c in range(min(DEC_SLOTS - 1, n_chunks)):
            for cp in page_copies(c, c):
                cp.start()

    @pl.when(ki == 0)
    def _():
        _attn_init(qt_ref, ft_ref, *state, maps)
        key = lax.broadcasted_iota(jnp.int32, (km_ref.shape[0], tq), 0)
        _attn_tile(km_ref, em_ref, vtm_ref, key < N_META, *state, maps)

    lam = _diff_lambda(lq1_ref, lk1_ref, lq2_ref, lk2_ref, lam_init)
    key = lax.broadcasted_iota(jnp.int32, (tk, tq), 0)
    qry = lax.broadcasted_iota(jnp.int32, (tk, tq), 1)
    mask = (key - qry) <= jnp.where(ki < qi, tk, 0)

    dec_in = (dqa_ref, dqb_ref, dkna_ref, dknb_ref, dvna_ref, dvnb_ref, dlfn_ref, sub_ref)
    dec_state = (dma_ref, dla_ref, dmb_ref, dlb_ref, dacca_ref, daccb_ref, dcarry_ref)

    @pl.when(g < n_dec_steps)
    def _():
        cst = _decode_consts(h_a, h_b, page)

        def hook(j):
            c = g * chunks_per_step + j
            slot = c % DEC_SLOTS
            for cp in page_copies(c, slot):
                cp.wait()
            ahead = c + DEC_SLOTS - 1

            @pl.when(ahead < n_chunks)
            def _():
                for cp in page_copies(ahead, ahead % DEC_SLOTS):
                    cp.start()

            _decode_chunk(c, slot, dec_in, (oa_ref, ob_ref), dec_state, pbuf, lfbuf, lam, cst,
                          n_pages=n_pages, page=page, h_a=h_a, h_b=h_b)

        _attn_tile(k_ref, e_ref, vt_ref, mask, *state, maps, hook=hook)

    @pl.when(g >= n_dec_steps)
    def _():
        _attn_tile(k_ref, e_ref, vt_ref, mask, *state, maps)

    @pl.when(ki == qi)
    def _():
        _attn_finalize(o_ref, osc_ref, lam, l_ref, acca_ref, accb_ref, h_a, h_b)


def _attn_meta_kernel(qt_ref, ft_ref, km_ref, em_ref, vtm_ref, lq1_ref, lk1_ref, lq2_ref, lk2_ref, osc_ref, o_ref,
                      qs_ref, m_ref, l_ref, acca_ref, accb_ref, *, h_a, h_b, lam_init):
    maps = _head_maps(h_a, h_b)
    state = (qs_ref, m_ref, l_ref, acca_ref, accb_ref)
    tk, tq = km_ref.shape[0], qt_ref.shape[1]
    _attn_init(qt_ref, ft_ref, *state, maps)
    key = lax.broadcasted_iota(jnp.int32, (tk, tq), 0)
    qry = lax.broadcasted_iota(jnp.int32, (tk, tq), 1)
    _attn_tile(km_ref, em_ref, vtm_ref, (key <= qry) & (key < N_META), *state, maps)
    lam = _diff_lambda(lq1_ref, lk1_ref, lq2_ref, lk2_ref, lam_init)
    _attn_finalize(o_ref, osc_ref, lam, l_ref, acca_ref, accb_ref, h_a, h_b)


def _attn_scratch(tq, h_a, h_b):
    n_maps = 2 * h_a + h_b
    return [pltpu.VMEM((n_maps * 2 * LANE, tq), BF16),
            pltpu.VMEM((n_maps, SUBLANES, tq), F32), pltpu.VMEM((n_maps, SUBLANES, tq), F32),
            pltpu.VMEM((2 * h_a * LANE, tq), F32), pltpu.VMEM((h_b * HEAD_DIM, tq), F32)]


def _attn_decode(qt, ft, k, e, vt, km, em, vtm, lams, osc, page_table, dec_small, sub, caches, *, batch, tile,
                 h_a, h_b, lam_init, layer):
    wq, r = qt.shape
    n = r // batch // tile
    db, n_pages = page_table.shape
    page = caches[4].shape[3]
    w_b = h_b * HEAD_DIM
    v_a = 2 * HEAD_DIM
    n_rows = 2 * h_a
    pairs = [(qi, ki) for qi in range(n) for ki in range(qi + 1)]
    q_tab = jnp.asarray([p[0] for p in pairs], jnp.int32)
    k_tab = jnp.asarray([p[1] for p in pairs], jnp.int32)
    n_chunks = db * n_pages // DEC_CHUNK
    chunks_per_step = DEC_HOOKS
    assert (2 * h_a + h_b) % DEC_HOOKS == 0
    chunks_per_seq = n_pages // DEC_CHUNK
    assert n_pages % DEC_CHUNK == 0 and chunks_per_seq % chunks_per_step == 0
    steps_per_seq = chunks_per_seq // chunks_per_step
    assert n_chunks // chunks_per_step <= batch * len(pairs)
    q_blk = lambda b, t, qtab, ktab, pt: b * n + qtab[t]
    k_blk = lambda b, t, qtab, ktab, pt: b * n + ktab[t]
    hbm = pl.BlockSpec(memory_space=pl.ANY)
    seq_of = lambda b, t, qtab, ktab, pt: jnp.minimum((b * len(pairs) + t) // steps_per_seq, db - 1)
    per_seq = lambda a: pl.BlockSpec((None,) + a.shape[1:], lambda *i: (seq_of(*i),) + (0,) * (a.ndim - 1))
    in_specs = [
        pl.BlockSpec((wq, tile), lambda *a: (0, q_blk(*a))),
        pl.BlockSpec((2 * LANE, tile), lambda *a: (0, q_blk(*a))),
        pl.BlockSpec((tile, wq), lambda *a: (k_blk(*a), 0)),
        pl.BlockSpec((tile, 2 * LANE), lambda *a: (k_blk(*a), 0)),
        pl.BlockSpec((wq, tile), lambda *a: (0, k_blk(*a))),
        _const_spec(km.shape), _const_spec(em.shape), _const_spec(vtm.shape),
    ] + [_const_spec(a.shape) for a in (*lams, osc)] + [per_seq(a) for a in dec_small] + [_const_spec(sub.shape)] \
        + [hbm] * 5
    kern = functools.partial(_attn_decode_kernel, h_a=h_a, h_b=h_b, lam_init=lam_init, layer=layer,
                             n_pages=n_pages, n_chunks=n_chunks, steps_per_batch=len(pairs))
    grid_spec = pltpu.PrefetchScalarGridSpec(
        num_scalar_prefetch=3,
        grid=(batch, len(pairs)),
        in_specs=in_specs,
        out_specs=[pl.BlockSpec((tile, wq), lambda *a: (q_blk(*a), 0)),
                   pl.BlockSpec((None, h_a, v_a), lambda *i: (seq_of(*i), 0, 0)),
                   pl.BlockSpec((None, 1, w_b), lambda *i: (seq_of(*i), 0, 0))],
        scratch_shapes=_attn_scratch(tile, h_a, h_b) + [
            pltpu.VMEM((DEC_SLOTS, 4, DEC_CHUNK, page * h_a, v_a), F32),
            pltpu.VMEM((DEC_SLOTS, DEC_CHUNK, h_b, page), F32),
            pltpu.SemaphoreType.DMA((DEC_SLOTS,)),
            pltpu.VMEM((n_rows, LANE), F32), pltpu.VMEM((n_rows, LANE), F32),
            pltpu.VMEM((h_b, LANE), F32), pltpu.VMEM((h_b, LANE), F32),
            pltpu.VMEM((n_rows, v_a), F32), pltpu.VMEM((h_b, w_b), F32), pltpu.VMEM((h_b, LANE), F32)],
    )
    return pl.pallas_call(
        kern,
        grid_spec=grid_spec,
        out_shape=[jax.ShapeDtypeStruct((r, wq), BF16), jax.ShapeDtypeStruct((db, h_a, v_a), F32),
                   jax.ShapeDtypeStruct((db, 1, w_b), F32)],
        compiler_params=_params(2),
        name="attn_decode",
    )(q_tab, k_tab, page_table, qt, ft, k, e, vt, km, em, vtm, *lams, osc, *dec_small, sub, *caches)


def _attn_meta(qt, ft, km, em, vtm, lams, osc, *, h_a, h_b, lam_init):
    wq, r = qt.shape
    args = (qt, ft, km, em, vtm, *lams, osc)
    kern = functools.partial(_attn_meta_kernel, h_a=h_a, h_b=h_b, lam_init=lam_init)
    return pl.pallas_call(
        kern,
        grid=(1,),
        in_specs=[_const_spec(a.shape) for a in args],
        out_specs=_const_spec((r, wq)),
        out_shape=jax.ShapeDtypeStruct((r, wq), BF16),
        scratch_shapes=_attn_scratch(r, h_a, h_b),
        compiler_params=_params(1),
        name="attn_meta",
    )(*args)


def _finish_kernel(x_ref, o_ref, wo_ref, g2_ref, wup_ref, wdn_ref, y_ref):
    x1 = x_ref[...] + jnp.dot(o_ref[...], wo_ref[...], preferred_element_type=F32)
    ms = jnp.mean(x1 * x1, axis=-1, keepdims=True)
    hn = (x1 * lax.rsqrt(ms + RMS_EPS) * g2_ref[...]).astype(BF16)
    acc = x1
    for c in range(0, wup_ref.shape[1], FF_CHUNK):
        u = jnp.maximum(jnp.dot(hn, wup_ref[:, c:c + FF_CHUNK], preferred_element_type=F32), 0.0)
        acc = acc + jnp.dot((u * u).astype(BF16), wdn_ref[c:c + FF_CHUNK, :], preferred_element_type=F32)
    y_ref[...] = acc


def _finish(x, o, w_o, g2, w_up, w_down, *, tr):
    r, d = x.shape
    rows = lambda w: pl.BlockSpec((tr, w), lambda i: (i, 0))
    return pl.pallas_call(
        _finish_kernel,
        grid=(r // tr,),
        in_specs=[rows(d), rows(o.shape[1]), _const_spec(w_o.shape), _const_spec(g2.shape),
                  _const_spec(w_up.shape), _const_spec(w_down.shape)],
        out_specs=rows(d),
        out_shape=jax.ShapeDtypeStruct((r, d), F32),
        compiler_params=_params(1),
        name="finish",
    )(x, o, w_o, g2, w_up, w_down)


def kernel(x_prompt, x_sample, cache_a_k, cache_a_v, cache_b_k, cache_b_v, cache_b_logf, page_table, meta_tokens,
           norm1, w_in, b_f, qn_a, kn_a, qn_b, kn_b, lam_q1, lam_k1, lam_q2, lam_k2, subln, w_o, norm2, w_up,
           w_down):
    batch, seq, d = x_prompt.shape
    db, dec_seq, _ = x_sample.shape
    depth, _, page, h_a, v_a = cache_a_k.shape
    h_b = cache_b_k.shape[3]
    w_a, w_b = h_a * v_a, h_b * HEAD_DIM
    wq = w_a + w_b
    n_small = LANE
    assert dec_seq == 1 and v_a == 2 * HEAD_DIM and meta_tokens.shape[0] == N_META
    assert w_a == w_b and w_a % MXU_DIM == 0 and h_a + h_b <= F_LANES
    assert SLOT * h_a <= LANE and SLOT * h_b <= LANE and N_META + db <= n_small
    assert seq % ATTN_TILE == 0 and seq % ROW_TILE == 0
    assert w_in.shape[2] == 3 * wq + h_b

    gmat = _group_sum_matrix()
    pmat, ones_row = _bias_placement(h_a, h_b)
    avec = np.zeros((1, F_LANES), np.float32)
    avec[0, h_b:h_b + h_a] = [s * LOG2E for s in _alibi_slopes(h_a)]
    avec = jnp.asarray(avec)
    q_scale = HEAD_DIM ** -0.5 * LOG2E

    n_pool = cache_a_k.shape[1]
    ak2 = cache_a_k.reshape(depth, n_pool, page * h_a, v_a)
    av2 = cache_a_v.reshape(depth, n_pool, page * h_a, v_a)
    bkt = cache_b_k.transpose(0, 1, 3, 4, 2).reshape(depth, n_pool, w_b, page)
    bvt = cache_b_v.transpose(0, 1, 3, 4, 2).reshape(depth, n_pool, w_b, page)
    lft = cache_b_logf.transpose(0, 1, 3, 2)

    x_main = x_prompt.reshape(batch * seq, d)
    x_small = jnp.concatenate([meta_tokens.astype(F32), x_sample.reshape(db, d),
                               jnp.zeros((n_small - N_META - db, d), F32)], axis=0)
    samples = slice(N_META, N_META + db)
    zero_c0 = jnp.zeros((1, F_LANES), F32)

    outs = {name: [] for name in ("p_ak", "p_av", "p_bk", "p_bv", "p_lf", "s_ak", "s_av", "s_bk", "s_bv", "s_lf")}
    for l in range(depth):
        lam_init = _lambda_init(l)
        w_main = w_in[l, :, :3 * wq].astype(BF16)
        w_f = jnp.pad(w_in[l, :, 3 * wq:], ((0, 0), (0, F_LANES - h_b))).astype(BF16)
        bias_f = jnp.pad(b_f[l], (0, F_LANES - h_b)).reshape(1, F_LANES)
        g1 = norm1[l].reshape(1, d)
        scales = jnp.stack([jnp.tile(qn_a[l], w_a // HEAD_DIM) * q_scale, jnp.tile(kn_a[l], w_a // HEAD_DIM),
                            jnp.tile(qn_b[l], w_b // HEAD_DIM) * q_scale, jnp.tile(kn_b[l], w_b // HEAD_DIM)])
        lams = tuple(a[l].reshape(1, HEAD_DIM) for a in (lam_q1, lam_k1, lam_q2, lam_k2))
        sub_row = (subln[l] * (1.0 - lam_init)).reshape(1, v_a)
        osc = jnp.concatenate([jnp.tile(sub_row, (1, h_a)), jnp.ones((1, w_b), F32)], axis=1)
        proj = functools.partial(_proj, g1=g1, w_main=w_main, w_f=w_f, b_f=bias_f, scales=scales, gmat=gmat,
                                 pmat=pmat, ones_row=ones_row, avec=avec, w_a=w_a, w_b=w_b, h_b=h_b)

        (ka_s, va_s, kb_s, vb_s, lf_s, cum_s, k_s, e_s, qt_s, vt_s, ft_s) = proj(
            x_small, c0=zero_c0, tr=n_small, tiles_per_seq=1, pos0=0)
        (ka_m, va_m, kb_m, vb_m, lf_m, _, k_m, e_m, qt_m, vt_m, ft_m) = proj(
            x_main, c0=cum_s[N_META - 1:N_META], tr=ROW_TILE, tiles_per_seq=seq // ROW_TILE, pos0=N_META)

        o_meta = _attn_meta(qt_s, ft_s, k_s, e_s, vt_s, lams, osc, h_a=h_a, h_b=h_b, lam_init=lam_init)
        q_rows = qt_s[:, samples].T.astype(F32)
        dec_small = (q_rows[:, :w_a].reshape(db, h_a, v_a), q_rows[:, w_a:].reshape(db, 1, w_b),
                     ka_s[samples].reshape(db, h_a, v_a), kb_s[samples].reshape(db, 1, w_b),
                     va_s[samples].reshape(db, h_a, v_a), vb_s[samples].reshape(db, 1, w_b),
                     lf_s[samples, :h_b].reshape(db, h_b, 1))
        o_main, oa_s, ob_s = _attn_decode(qt_m, ft_m, k_m, e_m, vt_m, k_s, e_s, vt_s, lams, osc, page_table,
                                          dec_small, sub_row, (ak2, av2, bkt, bvt, lft), batch=batch,
                                          tile=ATTN_TILE, h_a=h_a, h_b=h_b, lam_init=lam_init, layer=l)
        o_samples = jnp.concatenate([oa_s.reshape(db, w_a), ob_s.reshape(db, w_b)], axis=1).astype(BF16)
        o_small = jnp.concatenate([o_meta[:N_META], o_samples, jnp.zeros((n_small - N_META - db, wq), BF16)],
                                  axis=0)

        fin = functools.partial(_finish, w_o=w_o[l].astype(BF16), g2=norm2[l].reshape(1, d),
                                w_up=w_up[l].astype(BF16), w_down=w_down[l].astype(BF16))
        x_small = fin(x_small, o_small, tr=n_small)
        x_main = fin(x_main, o_main, tr=ROW_TILE)

        def prompt_rows(small, main):
            pieces = []
            for b in range(batch):
                pieces += [small[:N_META], main[b * seq:(b + 1) * seq]]
            return pieces

        outs["p_ak"] += prompt_rows(ka_s, ka_m)
        outs["p_av"] += prompt_rows(va_s, va_m)
        outs["p_bk"] += prompt_rows(kb_s, kb_m)
        outs["p_bv"] += prompt_rows(vb_s, vb_m)
        outs["p_lf"] += prompt_rows(lf_s[:, :h_b], lf_m[:, :h_b])
        outs["s_ak"].append(ka_s[samples].reshape(db, 1, h_a, v_a))
        outs["s_av"].append(va_s[samples].reshape(db, 1, h_a, v_a))
        outs["s_bk"].append(kb_s[samples].reshape(db, 1, h_b, HEAD_DIM))
        outs["s_bv"].append(vb_s[samples].reshape(db, 1, h_b, HEAD_DIM))
        outs["s_lf"].append(lf_s[samples, :h_b].reshape(db, 1, h_b))

    y_prompt = x_main.reshape(batch, seq, d)
    y_sample = x_small[samples].reshape(db, 1, d)
    stack = lambda name: jnp.stack(outs[name])
    rows = lambda name, tail: jnp.concatenate(outs[name], axis=0).reshape((depth, batch, N_META + seq) + tail)
    return (y_prompt, y_sample, rows("p_ak", (h_a, v_a)), rows("p_av", (h_a, v_a)), rows("p_bk", (h_b, HEAD_DIM)),
            rows("p_bv", (h_b, HEAD_DIM)), rows("p_lf", (h_b,)),
            stack("s_ak"), stack("s_av"), stack("s_bk"), stack("s_bv"), stack("s_lf"))
```
